```python
import jax, jax.numpy as jnp
from jax import lax
import numpy as np

D_MODEL = 1024
BATCH = 1
SEQ = 16384
DEPTH = 1
DEC_BATCH = 2
DEC_SEQ = 8192
PAST_LEN = 128

GRID_W = 64
EPS = 1e-6
M_HEADS = 4
M_HEAD_DIM = 128
M_WIDTH = M_HEADS * M_HEAD_DIM
M_CHUNK = 128
N_HEADS = 8
N_HEAD_DIM = 64
N_WIDTH = N_HEADS * N_HEAD_DIM
WIN_H = 8
WIN_W = 16
COL_BLOCK = 16
KEY_COL_SPAN = 32
PEER_HEADS = 8
N_SUBKEYS = 128
N_EXPERTS = N_SUBKEYS * N_SUBKEYS
PEER_KEY_DIM = 128
PEER_TOPK = 16
PEER_TOKEN_BLOCK = 128
SPLIT_POINTS = (3 * M_WIDTH, 4 * M_WIDTH, 4 * M_WIDTH + 4 * M_HEADS, 4 * M_WIDTH + 4 * M_HEADS + 3 * N_WIDTH)
IN_COLS = 4 * M_WIDTH + 4 * M_HEADS + 3 * N_WIDTH + 2 * D_MODEL

kernel_name = "hybrid_mlstm_natten2d_peer_encoder"


def rms_norm(x, g):
    xf = x.astype(jnp.float32)
    y = xf * lax.rsqrt(jnp.mean(xf * xf, axis=-1, keepdims=True) + EPS)
    return (y * g.astype(jnp.float32)).astype(x.dtype)


def mlstm_causal(q, k, v, li, lf):
    B, H, T, dh = q.shape
    L = M_CHUNK
    nc = T // L
    q = q.reshape(B, H, nc, L, dh)
    k = k.reshape(B, H, nc, L, dh)
    v = v.reshape(B, H, nc, L, dh)
    li = li.reshape(B, H, nc, L)
    lf = lf.reshape(B, H, nc, L)
    b = jnp.cumsum(lf, axis=-1)
    b_tot = b[..., -1]
    w_end = b_tot[..., None] - b + li
    m_loc = jnp.max(w_end, axis=-1)
    e_end = jnp.exp(w_end - m_loc[..., None])
    c_loc = jnp.einsum('bhcs,bhcsv,bhcsk->bhcvk', e_end, v, k)
    n_loc = jnp.einsum('bhcs,bhcsk->bhck', e_end, k)

    def step(carry, xs):
        c, n, m = carry
        bt, cl, nl, ml = xs
        m_new = jnp.maximum(bt + m, ml)
        a = jnp.exp(bt + m - m_new)
        bb = jnp.exp(ml - m_new)
        c_new = a[..., None, None] * c + bb[..., None, None] * cl
        n_new = a[..., None] * n + bb[..., None] * nl
        return (c_new, n_new, m_new), (c, n, m)

    init = (jnp.zeros((B, H, dh, dh), jnp.float32), jnp.zeros((B, H, dh), jnp.float32), jnp.zeros((B, H), jnp.float32))
    xs = (jnp.moveaxis(b_tot, 2, 0), jnp.moveaxis(c_loc, 2, 0), jnp.moveaxis(n_loc, 2, 0), jnp.moveaxis(m_loc, 2, 0))
    _, (c_prev, n_prev, m_prev) = lax.scan(step, init, xs)
    c_prev = jnp.moveaxis(c_prev, 0, 2)
    n_prev = jnp.moveaxis(n_prev, 0, 2)
    m_prev = jnp.moveaxis(m_prev, 0, 2)

    causal = jnp.tril(jnp.ones((L, L), dtype=bool))
    d = jnp.where(causal, b[..., :, None] - b[..., None, :] + li[..., None, :], -jnp.inf)
    a_inter = b + m_prev[..., None]
    m_t = jnp.maximum(a_inter, jnp.max(d, axis=-1))
    p = jnp.einsum('bhctd,bhcsd->bhcts', q, k) * jnp.exp(d - m_t[..., None])
    w_inter = jnp.exp(a_inter - m_t)
    num = jnp.einsum('bhcts,bhcsv->bhctv', p, v) + w_inter[..., None] * jnp.einsum('bhcvk,bhctk->bhctv', c_prev, q)
    den = jnp.sum(p, axis=-1) + w_inter * jnp.einsum('bhck,bhctk->bhct', n_prev, q)
    h = num / jnp.maximum(jnp.abs(den), jnp.exp(-m_t))[..., None]
    return h.reshape(B, H, T, dh)


def mlstm_branch(qkv, o_pre, gate_pre, b_i, b_f, g_head):
    B, T, _ = qkv.shape
    q, k, v = jnp.split(qkv.astype(jnp.float32), 3, axis=-1)
    to_heads = lambda t: t.reshape(B, T, M_HEADS, M_HEAD_DIM).transpose(0, 2, 1, 3)
    q = to_heads(q) * (M_HEAD_DIM ** -0.5)
    k = to_heads(k)
    v = to_heads(v)
    gates = gate_pre.astype(jnp.float32).reshape(B, T, 4, M_HEADS).transpose(0, 2, 3, 1)
    li = gates[:, 0:2] + b_i.astype(jnp.float32)[None, :, :, None]
    lf = jax.nn.log_sigmoid(gates[:, 2:4] + b_f.astype(jnp.float32)[None, :, :, None])
    h_fwd = mlstm_causal(q, k, v, li[:, 0], lf[:, 0])
    flip = lambda t: jnp.flip(t, axis=2)
    h_bwd = flip(mlstm_causal(flip(q), flip(k), flip(v), flip(li[:, 1]), flip(lf[:, 1])))
    h = (h_fwd + h_bwd).transpose(0, 2, 1, 3)
    h = h * lax.rsqrt(jnp.mean(h * h, axis=-1, keepdims=True) + EPS)
    h = h.reshape(B, T, M_WIDTH) * g_head.astype(jnp.float32)
    return (jax.nn.sigmoid(o_pre.astype(jnp.float32)) * h).astype(qkv.dtype)


def neighbourhood_attention(qkv, g_q, g_k, rpb):
    B, T, _ = qkv.shape
    rows = T // GRID_W
    kh = min(WIN_H, rows)
    q, k, v = jnp.split(qkv, 3, axis=-1)
    grid = lambda t: t.reshape(B, rows, GRID_W, N_HEADS, N_HEAD_DIM).transpose(0, 3, 1, 2, 4)
    q = rms_norm(grid(q), g_q).astype(jnp.float32) * (N_HEAD_DIM ** -0.5)
    k = rms_norm(grid(k), g_k).astype(jnp.float32)
    v = grid(v).astype(jnp.float32)
    r = jnp.arange(rows)
    r0 = jnp.clip(r - WIN_H // 2, 0, rows - kh)
    key_rows = r0[:, None] + jnp.arange(kh)[None, :]
    k_band = k[:, :, key_rows]
    v_band = v[:, :, key_rows]
    dr_idx = key_rows - r[:, None] + (WIN_H - 1)
    rpb32 = rpb.astype(jnp.float32)
    outs = []
    for j in range(GRID_W // COL_BLOCK):
        qc = np.arange(j * COL_BLOCK, (j + 1) * COL_BLOCK)
        c0 = np.clip(qc - WIN_W // 2, 0, GRID_W - WIN_W)
        kc_start = int(np.clip(j * COL_BLOCK - WIN_W // 2, 0, GRID_W - KEY_COL_SPAN))
        kc = kc_start + np.arange(KEY_COL_SPAN)
        col_mask = (kc[None, :] >= c0[:, None]) & (kc[None, :] < c0[:, None] + WIN_W)
        dc_idx = np.clip(kc[None, :] - qc[:, None] + WIN_W - 1, 0, 2 * WIN_W - 2)
        bias = rpb32[:, dr_idx[:, None, :, None], dc_idx[None, :, None, :]]
        qb = q[:, :, :, j * COL_BLOCK:(j + 1) * COL_BLOCK]
        kb = k_band[:, :, :, :, kc_start:kc_start + KEY_COL_SPAN]
        vb = v_band[:, :, :, :, kc_start:kc_start + KEY_COL_SPAN]
        s = jnp.einsum('bhrqd,bhrikd->bhrqik', qb, kb) + bias[None]
        s = jnp.where(col_mask[:, None, :], s, -jnp.inf)
        p = jax.nn.softmax(s, axis=(-2, -1))
        outs.append(jnp.einsum('bhrqik,bhrikd->bhrqd', p, vb))
    out = jnp.concatenate(outs, axis=3)
    return out.transpose(0, 2, 3, 1, 4).reshape(B, T, N_WIDTH).astype(qkv.dtype)


def peer(x, w_q, sub_keys, u_tab, v_tab):
    B, T, D = x.shape
    half = PEER_KEY_DIM // 2
    q = (x @ w_q).astype(jnp.float32).reshape(B, T, PEER_HEADS, 2, half)
    s = jnp.einsum('bthpd,hpnd->bthpn', q, sub_keys.astype(jnp.float32))
    top_s, top_i = lax.top_k(s, PEER_TOPK)
    cand_s = (top_s[..., 0, :, None] + top_s[..., 1, None, :]).reshape(B, T, PEER_HEADS, PEER_TOPK * PEER_TOPK)
    cand_i = (top_i[..., 0, :, None] * N_SUBKEYS + top_i[..., 1, None, :]).reshape(B, T, PEER_HEADS, PEER_TOPK * PEER_TOPK)
    best_s, pos = lax.top_k(cand_s, PEER_TOPK)
    idx = jnp.take_along_axis(cand_i, pos, axis=-1)
    g = jax.nn.softmax(best_s, axis=-1)
    nblk = (B * T) // PEER_TOKEN_BLOCK
    xs = (x.reshape(nblk, PEER_TOKEN_BLOCK, D),
          idx.reshape(nblk, PEER_TOKEN_BLOCK, PEER_HEADS, PEER_TOPK),
          g.reshape(nblk, PEER_TOKEN_BLOCK, PEER_HEADS, PEER_TOPK))

    def expert_block(args):
        xb, ib, gb = args
        a = jnp.einsum('thkd,td->thk', u_tab[ib], xb)
        hb = jax.nn.gelu(a.astype(jnp.float32)) * gb
        return jnp.einsum('thk,thkd->td', hb.astype(v_tab.dtype), v_tab[ib])

    y = lax.map(expert_block, xs)
    return y.reshape(B, T, D).astype(x.dtype)


def layer_forward(x, norm1_g, w_in, b_igate, b_fgate, mlstm_norm_g, qk_norm_q, qk_norm_k, rpb,
                  w_branch_a, w_branch_b, w_out, norm2_g, peer_wq, peer_subkeys, peer_u, peer_v):
    xn = rms_norm(x, norm1_g)
    proj = xn @ w_in
    m_qkv, m_o, m_gates, n_qkv, merge = jnp.split(proj, SPLIT_POINTS, axis=-1)
    y_a = mlstm_branch(m_qkv, m_o, m_gates, b_igate, b_fgate, mlstm_norm_g) @ w_branch_a
    y_b = neighbourhood_attention(n_qkv, qk_norm_q, qk_norm_k, rpb) @ w_branch_b
    gate_a, gate_b = jnp.split(jax.nn.sigmoid(merge.astype(jnp.float32)), 2, axis=-1)
    mixed = (gate_a * y_a.astype(jnp.float32) + gate_b * y_b.astype(jnp.float32)).astype(x.dtype) @ w_out
    h = x + mixed
    return h + peer(rms_norm(h, norm2_g), peer_wq, peer_subkeys, peer_u, peer_v)


def setup_inputs(seed: int = 0) -> dict:
    key = jax.random.key(seed)
    ks = jax.random.split(key, 20)
    f32 = jnp.float32
    nrm = lambda k, shape, scale: jax.random.normal(k, shape, f32) * scale
    f_bias = jnp.broadcast_to(jnp.linspace(3.0, 6.0, M_HEADS, dtype=f32), (DEPTH, 2, M_HEADS))
    return {
        'x_prompt': nrm(ks[0], (BATCH, SEQ, D_MODEL), 1.0),
        'x_sample': nrm(ks[1], (DEC_BATCH, DEC_SEQ, D_MODEL), 1.0),
        'norm1_g': 1.0 + nrm(ks[2], (DEPTH, D_MODEL), 0.02),
        'w_in': nrm(ks[3], (DEPTH, D_MODEL, IN_COLS), D_MODEL ** -0.5),
        'b_igate': nrm(ks[4], (DEPTH, 2, M_HEADS), 0.1),
        'b_fgate': f_bias + nrm(ks[5], (DEPTH, 2, M_HEADS), 0.1),
        'mlstm_norm_g': 1.0 + nrm(ks[6], (DEPTH, M_WIDTH), 0.02),
        'qk_norm_q': 1.0 + nrm(ks[7], (DEPTH, N_HEAD_DIM), 0.02),
        'qk_norm_k': 1.0 + nrm(ks[8], (DEPTH, N_HEAD_DIM), 0.02),
        'rpb': nrm(ks[9], (DEPTH, N_HEADS, 2 * WIN_H - 1, 2 * WIN_W - 1), 0.1),
        'w_branch_a': nrm(ks[10], (DEPTH, M_WIDTH, D_MODEL), M_WIDTH ** -0.5),
        'w_branch_b': nrm(ks[11], (DEPTH, N_WIDTH, D_MODEL), N_WIDTH ** -0.5),
        'w_out': nrm(ks[12], (DEPTH, D_MODEL, D_MODEL), D_MODEL ** -0.5),
        'norm2_g': 1.0 + nrm(ks[13], (DEPTH, D_MODEL), 0.02),
        'peer_wq': nrm(ks[14], (DEPTH, D_MODEL, PEER_HEADS * PEER_KEY_DIM), D_MODEL ** -0.5),
        'peer_subkeys': nrm(ks[15], (DEPTH, PEER_HEADS, 2, N_SUBKEYS, PEER_KEY_DIM // 2), (PEER_KEY_DIM // 2) ** -0.5),
        'peer_u': nrm(ks[16], (DEPTH, N_EXPERTS, D_MODEL), D_MODEL ** -0.5),
        'peer_v': nrm(ks[17], (DEPTH, N_EXPERTS, D_MODEL), PEER_TOPK ** -0.5),
    }


def reference(x_prompt, x_sample, norm1_g, w_in, b_igate, b_fgate, mlstm_norm_g, qk_norm_q, qk_norm_k, rpb,
              w_branch_a, w_branch_b, w_out, norm2_g, peer_wq, peer_subkeys, peer_u, peer_v):
    y_prompt = x_prompt
    y_sample = x_sample
    for l in range(DEPTH):
        params = (norm1_g[l], w_in[l], b_igate[l], b_fgate[l], mlstm_norm_g[l], qk_norm_q[l], qk_norm_k[l], rpb[l],
                  w_branch_a[l], w_branch_b[l], w_out[l], norm2_g[l], peer_wq[l], peer_subkeys[l], peer_u[l], peer_v[l])
        y_prompt = layer_forward(y_prompt, *params)
        y_sample = layer_forward(y_sample, *params)
    return (y_prompt, y_sample)
```

```python
import functools

import numpy as np
import jax
import jax.numpy as jnp
from jax import lax
from jax.experimental import pallas as pl
from jax.experimental.pallas import tpu as pltpu

F32 = jnp.float32
BF16 = jnp.bfloat16
I32 = jnp.int32

EPS = 1e-6
GRID_W = 64
M_HEADS = 4
M_HEAD_DIM = 128
M_WIDTH = M_HEADS * M_HEAD_DIM
M_CHUNK = 128
N_HEADS = 8
N_HEAD_DIM = 64
N_WIDTH = N_HEADS * N_HEAD_DIM
WIN_H = 8
WIN_W = 16
PEER_HEADS = 8
N_SUBKEYS = 128
PEER_KEY_DIM = 128
PEER_TOPK = 16
PEER_SLOTS = PEER_HEADS * PEER_TOPK

LANES = 128
VMEM_LIMIT = 48 * 1024 * 1024

NA_ROWS_PER_STEP = 8
NA_BLOCK = NA_ROWS_PER_STEP * GRID_W
NA_CLASSES = WIN_H
NEG_BIG = -1e30


def _nt_dot(a, b):
    return lax.dot_general(a, b, (((1,), (1,)), ((), ())), preferred_element_type=F32)


def _dot(a, b):
    return jnp.dot(a, b, preferred_element_type=F32)


def _split_bf16(a):
    hi = a.astype(BF16)
    lo = (a - hi.astype(F32)).astype(BF16)
    return hi, lo


def _proj_kernel(x_ref, g1_ref, wm_ref, wg_ref, wn_ref, wmg_ref, bd_ref, gq_ref, gk_ref,
                 mq_ref, mk_ref, mv_ref, mo_ref, gates_ref, nq_ref, nk_ref, nv_ref, merge_ref):
    x = x_ref[...]
    ms = jnp.mean(x * x, axis=-1, keepdims=True)
    xn = (x * lax.rsqrt(ms + EPS) * g1_ref[...]).astype(BF16)

    m = _dot(xn, wm_ref[...])
    mq_ref[...] = (m[:, 0:M_WIDTH] * (M_HEAD_DIM ** -0.5)).astype(BF16)
    mk_ref[...] = m[:, M_WIDTH:2 * M_WIDTH].astype(BF16)
    mv_ref[...] = m[:, 2 * M_WIDTH:3 * M_WIDTH].astype(BF16)
    mo_ref[...] = m[:, 3 * M_WIDTH:4 * M_WIDTH]
    gates_ref[...] = _dot(xn, wg_ref[...])

    n = _dot(xn, wn_ref[...])
    bd = bd_ref[...]

    def head_norm(t, g):
        hi, lo = _split_bf16(t * t)
        msq = _dot(hi, bd) + _dot(lo, bd)
        return t * lax.rsqrt(msq + EPS) * g

    nq_ref[...] = (head_norm(n[:, 0:N_WIDTH], gq_ref[...]) * (N_HEAD_DIM ** -0.5)).astype(BF16)
    nk_ref[...] = head_norm(n[:, N_WIDTH:2 * N_WIDTH], gk_ref[...]).astype(BF16)
    nv_ref[...] = n[:, 2 * N_WIDTH:3 * N_WIDTH].astype(BF16)
    merge_ref[...] = _dot(xn, wmg_ref[...])


def _proj(x, g1, wm, wg, wn, wmg, bd, gq, gk, tm):
    n_tok, d = x.shape
    grid = (n_tok // tm,)
    tok = lambda w: pl.BlockSpec((tm, w), lambda i: (i, 0))
    full = lambda a: pl.BlockSpec(a.shape, lambda i: (0,) * a.ndim)
    out_shape = (
        jax.ShapeDtypeStruct((n_tok, M_WIDTH), BF16),
        jax.ShapeDtypeStruct((n_tok, M_WIDTH), BF16),
        jax.ShapeDtypeStruct((n_tok, M_WIDTH), BF16),
        jax.ShapeDtypeStruct((n_tok, M_WIDTH), F32),
        jax.ShapeDtypeStruct((n_tok, LANES), F32),
        jax.ShapeDtypeStruct((n_tok, N_WIDTH), BF16),
        jax.ShapeDtypeStruct((n_tok, N_WIDTH), BF16),
        jax.ShapeDtypeStruct((n_tok, N_WIDTH), BF16),
        jax.ShapeDtypeStruct((n_tok, 2 * d), F32),
    )
    out_specs = (tok(M_WIDTH), tok(M_WIDTH), tok(M_WIDTH), tok(M_WIDTH), tok(LANES),
                 tok(N_WIDTH), tok(N_WIDTH), tok(N_WIDTH), tok(2 * d))
    return pl.pallas_call(
        _proj_kernel, grid=grid, out_shape=out_shape,
        in_specs=[tok(d), full(g1), full(wm), full(wg), full(wn), full(wmg), full(bd), full(gq), full(gk)],
        out_specs=out_specs, name="proj",
        compiler_params=pltpu.CompilerParams(dimension_semantics=("arbitrary",), vmem_limit_bytes=VMEM_LIMIT),
    )(x, g1, wm, wg, wn, wmg, bd, gq, gk)


def _log_sigmoid(z):
    return jnp.minimum(z, 0.0) - jnp.log1p(jnp.exp(-jnp.abs(z)))


def _mlstm_direction(q_ref, k_ref, v_ref, g_ref, bias_ref, out_ref, c_ref, n_ref, m_ref, direction):
    L = M_CHUNK
    g = g_ref[...] + bias_ref[...]
    lane = lax.broadcasted_iota(I32, (L, LANES), 1)
    lg = jnp.where(lane >= 2 * M_HEADS, _log_sigmoid(g), g)
    t_idx = lax.broadcasted_iota(I32, (L, L), 0)
    s_idx = lax.broadcasted_iota(I32, (L, L), 1)
    causal = (s_idx <= t_idx) if direction == 0 else (s_idx >= t_idx)
    tri = causal.astype(BF16)
    hi, lo = _split_bf16(lg)
    bc = _dot(tri, hi) + _dot(tri, lo)
    bt = bc.T
    lgt = lg.T
    for h in range(M_HEADS):
        u = direction * M_HEADS + h
        cf = 2 * M_HEADS + direction * M_HEADS + h
        ci = direction * M_HEADS + h
        bcol = bc[:, cf:cf + 1]
        brow = bt[cf:cf + 1, :]
        li_row = lgt[ci:ci + 1, :]
        li_col = lg[:, ci:ci + 1]
        btot = brow[:, L - 1:L] if direction == 0 else brow[:, 0:1]
        hs = slice(h * M_HEAD_DIM, (h + 1) * M_HEAD_DIM)
        q = q_ref[:, hs]
        k = k_ref[:, hs]
        v = v_ref[:, hs]
        m_prev = m_ref[u:u + 1, 0:1]
        c_prev = c_ref[u]
        n_prev = n_ref[u:u + 1, :]

        d = jnp.where(causal, bcol - brow + li_row, -jnp.inf)
        a_inter = bcol + m_prev
        m_t = jnp.maximum(a_inter, jnp.max(d, axis=1, keepdims=True))
        p = _nt_dot(q, k) * jnp.exp(d - m_t)
        w_inter = jnp.exp(a_inter - m_t)
        num = _dot(p.astype(BF16), v) + w_inter * _nt_dot(q, c_prev.astype(BF16))
        den = jnp.sum(p, axis=1, keepdims=True) + w_inter * jnp.sum(q.astype(F32) * n_prev, axis=1, keepdims=True)
        out_ref[:, hs] = num / jnp.maximum(jnp.abs(den), jnp.exp(-m_t))

        w_end = btot - bcol + li_col
        m_loc = jnp.max(w_end, axis=0, keepdims=True)
        m_new = jnp.maximum(btot + m_prev, m_loc)
        decay = jnp.exp(btot + m_prev - m_new)
        e = jnp.exp(w_end - m_new)
        ev = (e * v.astype(F32)).T.astype(BF16)
        c_ref[u] = decay * c_prev + _dot(ev, k)
        n_ref[u:u + 1, :] = decay * n_prev + jnp.sum(e * k.astype(F32), axis=0, keepdims=True)
        m_ref[u:u + 1, :] = jnp.broadcast_to(m_new, (1, LANES))


def _mlstm_kernel(qf, kf, vf, gf, qb, kb, vb, gb, bias_ref, hf_ref, hb_ref, c_ref, n_ref, m_ref, *, seq_chunks):
    j = pl.program_id(0)
    jb = pl.num_programs(0) - 1 - j
    start_f = functools.reduce(jnp.logical_or, [j == s for s, _ in seq_chunks])
    start_b = functools.reduce(jnp.logical_or, [jb == e - 1 for _, e in seq_chunks])

    def reset(lo):
        c_ref[lo:lo + M_HEADS] = jnp.zeros((M_HEADS, M_HEAD_DIM, M_HEAD_DIM), F32)
        n_ref[lo:lo + M_HEADS, :] = jnp.zeros((M_HEADS, LANES), F32)
        m_ref[lo:lo + M_HEADS, :] = jnp.zeros((M_HEADS, LANES), F32)

    @pl.when(start_f)
    def _():
        reset(0)

    @pl.when(start_b)
    def _():
        reset(M_HEADS)

    _mlstm_direction(qf, kf, vf, gf, bias_ref, hf_ref, c_ref, n_ref, m_ref, 0)
    _mlstm_direction(qb, kb, vb, gb, bias_ref, hb_ref, c_ref, n_ref, m_ref, 1)


def _mlstm(mq, mk, mv, gates, gate_bias, seq_chunks):
    n_tok = mq.shape[0]
    nchunk = n_tok // M_CHUNK
    fwd = lambda w: pl.BlockSpec((M_CHUNK, w), lambda j: (j, 0))
    bwd = lambda w: pl.BlockSpec((M_CHUNK, w), lambda j: (nchunk - 1 - j, 0))
    kern = functools.partial(_mlstm_kernel, seq_chunks=seq_chunks)
    return pl.pallas_call(
        kern, grid=(nchunk,),
        out_shape=(jax.ShapeDtypeStruct((n_tok, M_WIDTH), F32), jax.ShapeDtypeStruct((n_tok, M_WIDTH), F32)),
        in_specs=[fwd(M_WIDTH), fwd(M_WIDTH), fwd(M_WIDTH), fwd(LANES),
                  bwd(M_WIDTH), bwd(M_WIDTH), bwd(M_WIDTH), bwd(LANES),
                  pl.BlockSpec((1, LANES), lambda j: (0, 0))],
        out_specs=(fwd(M_WIDTH), bwd(M_WIDTH)),
        scratch_shapes=[pltpu.VMEM((2 * M_HEADS, M_HEAD_DIM, M_HEAD_DIM), F32),
                        pltpu.VMEM((2 * M_HEADS, LANES), F32),
                        pltpu.VMEM((2 * M_HEADS, LANES), F32)],
        name="mlstm",
        compiler_params=pltpu.CompilerParams(dimension_semantics=("arbitrary",), vmem_limit_bytes=VMEM_LIMIT),
    )(mq, mk, mv, gates, mq, mk, mv, gates, gate_bias)


def _seq_block_scalars(j, seq_blocks):
    start = jnp.int32(0)
    count = jnp.int32(seq_blocks[0][1] - seq_blocks[0][0])
    for s, e in seq_blocks[1:]:
        inside = j >= s
        start = jnp.where(inside, s, start)
        count = jnp.where(inside, e - s, count)
    return j - start, count


def _natten_kernel(q_ref, kp_ref, kc_ref, kn_ref, vp_ref, vc_ref, vn_ref, bias_ref, o_ref, kcat, vcat, *, seq_blocks):
    j = pl.program_id(1)
    jl, nb = _seq_block_scalars(j, seq_blocks)
    rows = nb * NA_ROWS_PER_STEP
    kcat[0:NA_BLOCK, :] = kp_ref[...]
    kcat[NA_BLOCK:2 * NA_BLOCK, :] = kc_ref[...]
    kcat[2 * NA_BLOCK:3 * NA_BLOCK, :] = kn_ref[...]
    vcat[0:NA_BLOCK, :] = vp_ref[...]
    vcat[NA_BLOCK:2 * NA_BLOCK, :] = vc_ref[...]
    vcat[2 * NA_BLOCK:3 * NA_BLOCK, :] = vn_ref[...]
    lane = lax.broadcasted_iota(I32, (GRID_W, LANES), 1)
    first_head = lane < N_HEAD_DIM

    def row_body(i, carry):
        r = jl * NA_ROWS_PER_STEP + i
        r0 = jnp.clip(r - WIN_H // 2, 0, rows - WIN_H)
        cls = r0 - r + (WIN_H - 1)
        ls = pl.multiple_of((r0 - jl * NA_ROWS_PER_STEP + NA_ROWS_PER_STEP) * GRID_W, GRID_W)
        qi = q_ref[pl.ds(pl.multiple_of(i * GRID_W, GRID_W), GRID_W), :]
        kband = kcat[pl.ds(ls, WIN_H * GRID_W), :]
        vband = vcat[pl.ds(ls, WIN_H * GRID_W), :]
        outs = []
        for hh in range(2):
            keep = first_head if hh == 0 else jnp.logical_not(first_head)
            qm = jnp.where(keep, qi, jnp.zeros_like(qi))
            s = _nt_dot(qm, kband) + bias_ref[hh, cls]
            mx = jnp.max(s, axis=1, keepdims=True)
            p = jnp.exp(s - mx)
            den = jnp.sum(p, axis=1, keepdims=True)
            outs.append(_dot(p.astype(BF16), vband) / den)
        o_ref[pl.ds(pl.multiple_of(i * GRID_W, GRID_W), GRID_W), :] = jnp.where(first_head, outs[0], outs[1]).astype(o_ref.dtype)
        return carry

    lax.fori_loop(0, NA_ROWS_PER_STEP, row_body, 0)


def _natten(nq, nk, nv, bias_tab, seq_blocks):
    n_tok = nq.shape[0]
    nblk = n_tok // NA_BLOCK
    npair = N_HEADS // 2

    def nbr_map(delta):
        def index_map(p, j):
            lo = jnp.int32(seq_blocks[0][0])
            hi = jnp.int32(seq_blocks[0][1] - 1)
            for s, e in seq_blocks[1:]:
                inside = j >= s
                lo = jnp.where(inside, s, lo)
                hi = jnp.where(inside, e - 1, hi)
            return (jnp.clip(j + delta, lo, hi), p)
        return index_map

    blk = lambda im: pl.BlockSpec((NA_BLOCK, LANES), im)
    cur = lambda p, j: (j, p)
    kern = functools.partial(_natten_kernel, seq_blocks=seq_blocks)
    return pl.pallas_call(
        kern, grid=(npair, nblk),
        out_shape=jax.ShapeDtypeStruct((n_tok, N_WIDTH), BF16),
        in_specs=[blk(cur), blk(nbr_map(-1)), blk(cur), blk(nbr_map(1)),
                  blk(nbr_map(-1)), blk(cur), blk(nbr_map(1)),
                  pl.BlockSpec((2, NA_CLASSES, GRID_W, WIN_H * GRID_W), lambda p, j: (p, 0, 0, 0))],
        out_specs=blk(cur),
        scratch_shapes=[pltpu.VMEM((3 * NA_BLOCK, LANES), BF16), pltpu.VMEM((3 * NA_BLOCK, LANES), BF16)],
        name="natten",
        compiler_params=pltpu.CompilerParams(dimension_semantics=("arbitrary", "arbitrary"), vmem_limit_bytes=VMEM_LIMIT),
    )(nq, nk, nk, nk, nv, nv, nv, bias_tab)


def _natten_bias_table(rpb):
    c = np.arange(GRID_W)
    c0 = np.clip(c - WIN_W // 2, 0, GRID_W - WIN_W)
    kc = np.arange(GRID_W)
    valid = (kc[None, :] >= c0[:, None]) & (kc[None, :] < c0[:, None] + WIN_W)
    dc = np.clip(kc[None, :] - c[:, None] + WIN_W - 1, 0, 2 * WIN_W - 2)
    dr = np.arange(NA_CLASSES)[:, None] + np.arange(WIN_H)[None, :]
    tab = rpb.astype(F32)[:, dr[:, :, None, None], dc[None, None, :, :]]
    tab = jnp.where(valid[None, None, None], tab, NEG_BIG)
    tab = tab.transpose(0, 1, 3, 2, 4)
    return tab.reshape(N_HEADS, NA_CLASSES, GRID_W, WIN_H * GRID_W)


def _mix_kernel(x_ref, hf_ref, hb_ref, mo_ref, na_ref, mg_ref, gh_ref, wa_ref, wb_ref, wo_ref, g2_ref, wq_ref, sk_ref,
                h_ref, hn_ref, st_ref):
    d = x_ref.shape[1]
    hm = hf_ref[...] + hb_ref[...]
    parts = []
    for h in range(M_HEADS):
        hs = hm[:, h * M_HEAD_DIM:(h + 1) * M_HEAD_DIM]
        parts.append(hs * lax.rsqrt(jnp.mean(hs * hs, axis=-1, keepdims=True) + EPS))
    hm = jnp.concatenate(parts, axis=1) * gh_ref[...]
    ya = _dot((jax.nn.sigmoid(mo_ref[...]) * hm).astype(BF16), wa_ref[...])
    yb = _dot(na_ref[...], wb_ref[...])
    mg = mg_ref[...]
    mixed = jax.nn.sigmoid(mg[:, 0:d]) * ya + jax.nn.sigmoid(mg[:, d:2 * d]) * yb
    hres = x_ref[...] + _dot(mixed.astype(BF16), wo_ref[...])
    h_ref[...] = hres
    hn = hres * lax.rsqrt(jnp.mean(hres * hres, axis=-1, keepdims=True) + EPS) * g2_ref[...]
    hn_ref[...] = hn
    qp = _dot(hn.astype(BF16), wq_ref[...]).astype(BF16)
    for b in range(2 * PEER_HEADS):
        lb = b // 2
        st_ref[b] = _nt_dot(sk_ref[b], qp[:, lb * LANES:(lb + 1) * LANES])


def _mix(x, hf, hb, mo, na, merge, gh, wa, wb, wo, g2, wq, skp, tm):
    n_tok, d = x.shape
    tok = lambda w: pl.BlockSpec((tm, w), lambda i: (i, 0))
    full = lambda a: pl.BlockSpec(a.shape, lambda i: (0,) * a.ndim)
    nsk = 2 * PEER_HEADS
    return pl.pallas_call(
        _mix_kernel, grid=(n_tok // tm,),
        out_shape=(jax.ShapeDtypeStruct((n_tok, d), F32), jax.ShapeDtypeStruct((n_tok, d), F32),
                   jax.ShapeDtypeStruct((nsk, N_SUBKEYS, n_tok), F32)),
        in_specs=[tok(d), tok(M_WIDTH), tok(M_WIDTH), tok(M_WIDTH), tok(N_WIDTH), tok(2 * d),
                  full(gh), full(wa), full(wb), full(wo), full(g2), full(wq), full(skp)],
        out_specs=(tok(d), tok(d), pl.BlockSpec((nsk, N_SUBKEYS, tm), lambda i: (0, 0, i))),
        name="mix",
        compiler_params=pltpu.CompilerParams(dimension_semantics=("arbitrary",), vmem_limit_bytes=VMEM_LIMIT),
    )(x, hf, hb, mo, na, merge, gh, wa, wb, wo, g2, wq, skp)


CAND_ROWS = PEER_TOPK + 7 * 8 + 8


def _cand_constants(tt):
    ii = np.concatenate([np.zeros(16), np.repeat(np.arange(1, 8), 8), np.arange(8, 16)]).astype(np.int64)
    jj = np.concatenate([np.arange(16), np.tile(np.arange(8), 7), np.zeros(8)]).astype(np.int64)
    valid = (ii + 1) * (jj + 1) <= PEER_TOPK
    pos = (ii * PEER_TOPK + jj).astype(np.int32)
    mask = np.where(valid, 0.0, -np.inf).astype(np.float32)
    return (jnp.asarray(np.broadcast_to(pos[:, None], (CAND_ROWS, tt))),
            jnp.asarray(np.broadcast_to(mask[:, None], (CAND_ROWS, tt))))


def _topk_kernel(s_ref, pos_ref, mask_ref, idx_ref, g_ref, va, ia, vb, ib, ids, gts):
    tt = s_ref.shape[2]
    iota = lax.broadcasted_iota(I32, (N_SUBKEYS, tt), 0)
    cpos = pos_ref[...]
    cmask = mask_ref[...]

    def head_body(h, carry):
        for half, (vref, iref) in enumerate(((va, ia), (vb, ib))):
            s = s_ref[2 * h + half]
            for kk in range(PEER_TOPK):
                m = jnp.max(s, axis=0, keepdims=True)
                pos = jnp.min(jnp.where(s == m, iota, N_SUBKEYS), axis=0, keepdims=True)
                vref[kk:kk + 1, :] = m
                iref[kk:kk + 1, :] = pos
                s = jnp.where(iota == pos, -jnp.inf, s)
        a_lo, a_hi = va[0:8, :], va[8:16, :]
        ia_lo, ia_hi = ia[0:8, :], ia[8:16, :]
        b16, b8 = vb[...], vb[0:8, :]
        ib16, ib8 = ib[...], ib[0:8, :]
        cand = [a_lo[0:1, :] + b16]
        cid = [ia_lo[0:1, :] * N_SUBKEYS + ib16]
        for i in range(1, 8):
            cand.append(a_lo[i:i + 1, :] + b8)
            cid.append(ia_lo[i:i + 1, :] * N_SUBKEYS + ib8)
        cand.append(a_hi + b8[0:1, :])
        cid.append(ia_hi * N_SUBKEYS + ib8[0:1, :])
        cand = jnp.concatenate(cand, axis=0) + cmask
        cid = jnp.concatenate(cid, axis=0)
        best = []
        row0 = pl.multiple_of(h * PEER_TOPK, PEER_TOPK)
        for kk in range(PEER_TOPK):
            m = jnp.max(cand, axis=0, keepdims=True)
            pos = jnp.min(jnp.where(cand == m, cpos, PEER_TOPK * PEER_TOPK), axis=0, keepdims=True)
            sel = cpos == pos
            best.append(m)
            ids[pl.ds(row0 + kk, 1), :] = jnp.max(jnp.where(sel, cid, -1), axis=0, keepdims=True)
            cand = jnp.where(sel, -jnp.inf, cand)
        best = jnp.concatenate(best, axis=0)
        ex = jnp.exp(best - best[0:1, :])
        gts[pl.ds(row0, PEER_TOPK), :] = ex / jnp.sum(ex, axis=0, keepdims=True)
        return carry

    lax.fori_loop(0, PEER_HEADS, head_body, 0)
    idx_ref[...] = ids[...].T
    g_ref[...] = gts[...].T


def _topk(scores_t, tt):
    nsk, _, n_tok = scores_t.shape
    cpos, cmask = _cand_constants(tt)
    const = pl.BlockSpec((CAND_ROWS, tt), lambda i: (0, 0))
    return pl.pallas_call(
        _topk_kernel, grid=(n_tok // tt,),
        out_shape=(jax.ShapeDtypeStruct((n_tok, PEER_SLOTS), I32), jax.ShapeDtypeStruct((n_tok, PEER_SLOTS), F32)),
        in_specs=[pl.BlockSpec((nsk, N_SUBKEYS, tt), lambda i: (0, 0, i)), const, const],
        out_specs=(pl.BlockSpec((tt, PEER_SLOTS), lambda i: (i, 0)), pl.BlockSpec((tt, PEER_SLOTS), lambda i: (i, 0))),
        scratch_shapes=[pltpu.VMEM((PEER_TOPK, tt), F32), pltpu.VMEM((PEER_TOPK, tt), I32),
                        pltpu.VMEM((PEER_TOPK, tt), F32), pltpu.VMEM((PEER_TOPK, tt), I32),
                        pltpu.VMEM((PEER_SLOTS, tt), I32), pltpu.VMEM((PEER_SLOTS, tt), F32)],
        name="topk",
        compiler_params=pltpu.CompilerParams(dimension_semantics=("arbitrary",), vmem_limit_bytes=VMEM_LIMIT),
    )(scores_t, cpos, cmask)


EXPERT_TOKENS = 8
GATHER_UNROLL = 8


def _experts_kernel(idx_ref, idxn_ref, hn_ref, h_ref, g_ref, uv_hbm, o_ref, buf, sem):
    s = pl.program_id(0)
    ns = pl.num_programs(0)
    slot = lax.rem(s, 2)
    d = hn_ref.shape[1]
    rows = EXPERT_TOKENS * PEER_SLOTS

    def row_copy(e, dst_slot, r):
        return pltpu.make_async_copy(uv_hbm.at[pl.ds(e, 1), :], buf.at[dst_slot, pl.ds(r, 1), :], sem.at[dst_slot])

    def issue(ids_ref, dst_slot):
        for t in range(EXPERT_TOKENS):
            def body(c, carry):
                for uu in range(GATHER_UNROLL):
                    kk = c * GATHER_UNROLL + uu
                    row_copy(ids_ref[t, kk], dst_slot, t * PEER_SLOTS + kk).start()
                return carry
            lax.fori_loop(0, PEER_SLOTS // GATHER_UNROLL, body, 0)

    @pl.when(s == 0)
    def _():
        issue(idx_ref, slot)

    @pl.when(s + 1 < ns)
    def _():
        issue(idxn_ref, 1 - slot)

    pltpu.make_async_copy(uv_hbm.at[pl.ds(0, rows), :], buf.at[slot], sem.at[slot]).wait()

    gt = g_ref[...].T
    for t in range(EXPERT_TOKENS):
        rs = slice(t * PEER_SLOTS, (t + 1) * PEER_SLOTS)
        u = buf[slot, rs, 0:d]
        a = jnp.sum(u * hn_ref[t:t + 1, :], axis=1, keepdims=True)
        hcoef = jax.nn.gelu(a) * gt[:, t:t + 1]
        v = buf[slot, rs, d:2 * d]
        o_ref[t:t + 1, :] = h_ref[t:t + 1, :] + jnp.sum(hcoef * v, axis=0, keepdims=True)


def _experts(idx, hn, hres, gates, uv):
    n_tok, d = hn.shape
    nstep = n_tok // EXPERT_TOKENS
    tok = pl.BlockSpec((EXPERT_TOKENS, d), lambda s: (s, 0))
    smem_idx = lambda im: pl.BlockSpec((EXPERT_TOKENS, PEER_SLOTS), im, memory_space=pltpu.SMEM)
    return pl.pallas_call(
        _experts_kernel, grid=(nstep,),
        out_shape=jax.ShapeDtypeStruct((n_tok, d), F32),
        in_specs=[smem_idx(lambda s: (s, 0)), smem_idx(lambda s: (jnp.minimum(s + 1, nstep - 1), 0)),
                  tok, tok, pl.BlockSpec((EXPERT_TOKENS, PEER_SLOTS), lambda s: (s, 0)),
                  pl.BlockSpec(memory_space=pl.ANY)],
        out_specs=tok,
        scratch_shapes=[pltpu.VMEM((2, EXPERT_TOKENS * PEER_SLOTS, 2 * d), F32), pltpu.SemaphoreType.DMA((2,))],
        name="experts",
        compiler_params=pltpu.CompilerParams(dimension_semantics=("arbitrary",), vmem_limit_bytes=VMEM_LIMIT),
    )(idx, idx, hn, hres, gates, uv)


def _layer(x, seqs, norm1_g, w_in, b_igate, b_fgate, mlstm_norm_g, qk_norm_q, qk_norm_k, rpb,
           w_branch_a, w_branch_b, w_out, norm2_g, peer_wq, peer_subkeys, peer_u, peer_v):
    n_tok, d = x.shape
    tm = 256
    row = lambda a: a.astype(F32).reshape(1, -1)

    c0, c1, c2, c3 = 3 * M_WIDTH, 4 * M_WIDTH, 4 * M_WIDTH + 4 * M_HEADS, 4 * M_WIDTH + 4 * M_HEADS + 3 * N_WIDTH
    wm = w_in[:, 0:c1].astype(BF16)
    wg = jnp.pad(w_in[:, c1:c2], ((0, 0), (0, LANES - 4 * M_HEADS))).astype(BF16)
    wn = w_in[:, c2:c3].astype(BF16)
    wmg = w_in[:, c3:].astype(BF16)
    head_of = np.arange(N_WIDTH) // N_HEAD_DIM
    bd = jnp.asarray((head_of[:, None] == head_of[None, :]).astype(np.float32) / N_HEAD_DIM, BF16)
    gq = jnp.tile(qk_norm_q.astype(F32), N_HEADS).reshape(1, N_WIDTH)
    gk = jnp.tile(qk_norm_k.astype(F32), N_HEADS).reshape(1, N_WIDTH)
    gate_bias = jnp.pad(jnp.concatenate([b_igate.reshape(-1), b_fgate.reshape(-1)]).astype(F32),
                        (0, LANES - 4 * M_HEADS)).reshape(1, LANES)
    bias_tab = _natten_bias_table(rpb)
    half = PEER_KEY_DIM // 2
    sk = peer_subkeys.astype(F32).reshape(2 * PEER_HEADS, N_SUBKEYS, half)
    z = jnp.zeros_like(sk)
    odd = (np.arange(2 * PEER_HEADS) % 2 == 1)[:, None, None]
    skp = jnp.where(odd, jnp.concatenate([z, sk], axis=-1), jnp.concatenate([sk, z], axis=-1)).astype(BF16)
    uv = jnp.concatenate([peer_u, peer_v], axis=1)

    seq_chunks = tuple((s // M_CHUNK, (s + t) // M_CHUNK) for s, t in seqs)
    seq_blocks = tuple((s // NA_BLOCK, (s + t) // NA_BLOCK) for s, t in seqs)

    mq, mk, mv, mo, gates, nq, nk, nv, merge = _proj(x, row(norm1_g), wm, wg, wn, wmg, bd, gq, gk, tm)
    hf, hb = _mlstm(mq, mk, mv, gates, gate_bias, seq_chunks)
    na = _natten(nq, nk, nv, bias_tab, seq_blocks)
    hres, hn, scores_t = _mix(x, hf, hb, mo, na, merge, row(mlstm_norm_g), w_branch_a.astype(BF16),
                              w_branch_b.astype(BF16), w_out.astype(BF16), row(norm2_g), peer_wq.astype(BF16), skp, tm)
    idx, pg = _topk(scores_t, LANES)
    return _experts(idx, hn, hres, pg, uv)


def kernel(x_prompt, x_sample, norm1_g, w_in, b_igate, b_fgate, mlstm_norm_g, qk_norm_q, qk_norm_k, rpb,
           w_branch_a, w_branch_b, w_out, norm2_g, peer_wq, peer_subkeys, peer_u, peer_v):
    depth = w_in.shape[0]
    d = x_prompt.shape[-1]
    groups = (x_prompt, x_sample)
    seqs, start = [], 0
    for gx in groups:
        for _ in range(gx.shape[0]):
            seqs.append((start, gx.shape[1]))
            start += gx.shape[1]
    for s, t in seqs:
        assert t % NA_BLOCK == 0 and t % M_CHUNK == 0 and t // GRID_W >= WIN_H
    x = jnp.concatenate([gx.reshape(-1, d) for gx in groups], axis=0)
    for l in range(depth):
        x = _layer(x, tuple(seqs), norm1_g[l], w_in[l], b_igate[l], b_fgate[l], mlstm_norm_g[l], qk_norm_q[l],
                   qk_norm_k[l], rpb[l], w_branch_a[l], w_branch_b[l], w_out[l], norm2_g[l], peer_wq[l],
                   peer_subkeys[l], peer_u[l], peer_v[l])
    outs, start = [], 0
    for gx in groups:
        cnt = gx.shape[0] * gx.shape[1]
        outs.append(x[start:start + cnt].reshape(gx.shape))
        start += cnt
    return tuple(outs)
```

```python
import functools

import numpy as np
import jax
import jax.numpy as jnp
from jax import lax
from jax.experimental import pallas as pl
from jax.experimental.pallas import tpu as pltpu

F32 = jnp.float32
BF16 = jnp.bfloat16
I32 = jnp.int32

EPS = 1e-6
GRID_W = 64
M_HEADS = 4
M_HEAD_DIM = 128
M_WIDTH = M_HEADS * M_HEAD_DIM
M_CHUNK = 128
N_HEADS = 8
N_HEAD_DIM = 64
N_WIDTH = N_HEADS * N_HEAD_DIM
WIN_H = 8
WIN_W = 16
PEER_HEADS = 8
N_SUBKEYS = 128
PEER_KEY_DIM = 128
PEER_TOPK = 16
PEER_SLOTS = PEER_HEADS * PEER_TOPK

LANES = 128
VMEM_LIMIT = 48 * 1024 * 1024

NA_ROWS_PER_STEP = 8
NA_BLOCK = NA_ROWS_PER_STEP * GRID_W
NA_CLASSES = WIN_H
NEG_BIG = -1e30


def _nt_dot(a, b):
    return lax.dot_general(a, b, (((1,), (1,)), ((), ())), preferred_element_type=F32)


def _dot(a, b):
    return jnp.dot(a, b, preferred_element_type=F32)


def _split_bf16(a):
    hi = a.astype(BF16)
    lo = (a - hi.astype(F32)).astype(BF16)
    return hi, lo


def _proj_kernel(x_ref, g1_ref, wm_ref, wg_ref, wn_ref, wmg_ref, bd_ref, gq_ref, gk_ref,
                 mq_ref, mk_ref, mv_ref, mo_ref, gates_ref, nq_ref, nk_ref, nv_ref, merge_ref):
    x = x_ref[...]
    ms = jnp.mean(x * x, axis=-1, keepdims=True)
    xn = (x * lax.rsqrt(ms + EPS) * g1_ref[...]).astype(BF16)

    m = _dot(xn, wm_ref[...])
    mq_ref[...] = (m[:, 0:M_WIDTH] * (M_HEAD_DIM ** -0.5)).astype(BF16)
    mk_ref[...] = m[:, M_WIDTH:2 * M_WIDTH].astype(BF16)
    mv_ref[...] = m[:, 2 * M_WIDTH:3 * M_WIDTH].astype(BF16)
    mo_ref[...] = m[:, 3 * M_WIDTH:4 * M_WIDTH]
    gates_ref[...] = _dot(xn, wg_ref[...])

    n = _dot(xn, wn_ref[...])
    bd = bd_ref[...]

    def head_norm(t, g):
        hi, lo = _split_bf16(t * t)
        msq = _dot(hi, bd) + _dot(lo, bd)
        return t * lax.rsqrt(msq + EPS) * g

    nq_ref[...] = (head_norm(n[:, 0:N_WIDTH], gq_ref[...]) * (N_HEAD_DIM ** -0.5)).astype(BF16)
    nk_ref[...] = head_norm(n[:, N_WIDTH:2 * N_WIDTH], gk_ref[...]).astype(BF16)
    nv_ref[...] = n[:, 2 * N_WIDTH:3 * N_WIDTH].astype(BF16)
    merge_ref[...] = _dot(xn, wmg_ref[...])


def _proj(x, g1, wm, wg, wn, wmg, bd, gq, gk, tm):
    n_tok, d = x.shape
    grid = (n_tok // tm,)
    tok = lambda w: pl.BlockSpec((tm, w), lambda i: (i, 0))
    full = lambda a: pl.BlockSpec(a.shape, lambda i: (0,) * a.ndim)
    out_shape = (
        jax.ShapeDtypeStruct((n_tok, M_WIDTH), BF16),
        jax.ShapeDtypeStruct((n_tok, M_WIDTH), BF16),
        jax.ShapeDtypeStruct((n_tok, M_WIDTH), BF16),
        jax.ShapeDtypeStruct((n_tok, M_WIDTH), F32),
        jax.ShapeDtypeStruct((n_tok, LANES), F32),
        jax.ShapeDtypeStruct((n_tok, N_WIDTH), BF16),
        jax.ShapeDtypeStruct((n_tok, N_WIDTH), BF16),
        jax.ShapeDtypeStruct((n_tok, N_WIDTH), BF16),
        jax.ShapeDtypeStruct((n_tok, 2 * d), F32),
    )
    out_specs = (tok(M_WIDTH), tok(M_WIDTH), tok(M_WIDTH), tok(M_WIDTH), tok(LANES),
                 tok(N_WIDTH), tok(N_WIDTH), tok(N_WIDTH), tok(2 * d))
    return pl.pallas_call(
        _proj_kernel, grid=grid, out_shape=out_shape,
        in_specs=[tok(d), full(g1), full(wm), full(wg), full(wn), full(wmg), full(bd), full(gq), full(gk)],
        out_specs=out_specs, name="proj",
        compiler_params=pltpu.CompilerParams(dimension_semantics=("arbitrary",), vmem_limit_bytes=VMEM_LIMIT),
    )(x, g1, wm, wg, wn, wmg, bd, gq, gk)


def _log_sigmoid(z):
    return jnp.minimum(z, 0.0) - jnp.log1p(jnp.exp(-jnp.abs(z)))


def _mlstm_direction(q_ref, k_ref, v_ref, g_ref, bias_ref, out_ref, c_ref, n_ref, m_ref, direction):
    L = M_CHUNK
    g = g_ref[...] + bias_ref[...]
    lane = lax.broadcasted_iota(I32, (L, LANES), 1)
    lg = jnp.where(lane >= 2 * M_HEADS, _log_sigmoid(g), g)
    t_idx = lax.broadcasted_iota(I32, (L, L), 0)
    s_idx = lax.broadcasted_iota(I32, (L, L), 1)
    causal = (s_idx <= t_idx) if direction == 0 else (s_idx >= t_idx)
    tri = causal.astype(BF16)
    hi, lo = _split_bf16(lg)
    bc = _dot(tri, hi) + _dot(tri, lo)
    bt = bc.T
    lgt = lg.T
    for h in range(M_HEADS):
        u = direction * M_HEADS + h
        cf = 2 * M_HEADS + direction * M_HEADS + h
        ci = direction * M_HEADS + h
        bcol = bc[:, cf:cf + 1]
        brow = bt[cf:cf + 1, :]
        li_row = lgt[ci:ci + 1, :]
        li_col = lg[:, ci:ci + 1]
        btot = brow[:, L - 1:L] if direction == 0 else brow[:, 0:1]
        hs = slice(h * M_HEAD_DIM, (h + 1) * M_HEAD_DIM)
        q = q_ref[:, hs]
        k = k_ref[:, hs]
        v = v_ref[:, hs]
        m_prev = m_ref[u:u + 1, 0:1]
        c_prev = c_ref[u]
        n_prev = n_ref[u:u + 1, :]

        d = jnp.where(causal, bcol - brow + li_row, -jnp.inf)
        a_inter = bcol + m_prev
        m_t = jnp.maximum(a_inter, jnp.max(d, axis=1, keepdims=True))
        p = _nt_dot(q, k) * jnp.exp(d - m_t)
        w_inter = jnp.exp(a_inter - m_t)
        num = _dot(p.astype(BF16), v) + w_inter * _nt_dot(q, c_prev.astype(BF16))
        den = jnp.sum(p, axis=1, keepdims=True) + w_inter * jnp.sum(q.astype(F32) * n_prev, axis=1, keepdims=True)
        out_ref[:, hs] = num / jnp.maximum(jnp.abs(den), jnp.exp(-m_t))

        w_end = btot - bcol + li_col
        m_loc = jnp.max(w_end, axis=0, keepdims=True)
        m_new = jnp.maximum(btot + m_prev, m_loc)
        decay = jnp.exp(btot + m_prev - m_new)
        e = jnp.exp(w_end - m_new)
        ev = (e * v.astype(F32)).T.astype(BF16)
        c_ref[u] = decay * c_prev + _dot(ev, k)
        n_ref[u:u + 1, :] = decay * n_prev + jnp.sum(e * k.astype(F32), axis=0, keepdims=True)
        m_ref[u:u + 1, :] = jnp.broadcast_to(m_new, (1, LANES))


def _mlstm_kernel(qf, kf, vf, gf, qb, kb, vb, gb, bias_ref, hf_ref, hb_ref, c_ref, n_ref, m_ref, *, seq_chunks):
    j = pl.program_id(0)
    jb = pl.num_programs(0) - 1 - j
    start_f = functools.reduce(jnp.logical_or, [j == s for s, _ in seq_chunks])
    start_b = functools.reduce(jnp.logical_or, [jb == e - 1 for _, e in seq_chunks])

    def reset(lo):
        c_ref[lo:lo + M_HEADS] = jnp.zeros((M_HEADS, M_HEAD_DIM, M_HEAD_DIM), F32)
        n_ref[lo:lo + M_HEADS, :] = jnp.zeros((M_HEADS, LANES), F32)
        m_ref[lo:lo + M_HEADS, :] = jnp.zeros((M_HEADS, LANES), F32)

    @pl.when(start_f)
    def _():
        reset(0)

    @pl.when(start_b)
    def _():
        reset(M_HEADS)

    _mlstm_direction(qf, kf, vf, gf, bias_ref, hf_ref, c_ref, n_ref, m_ref, 0)
    _mlstm_direction(qb, kb, vb, gb, bias_ref, hb_ref, c_ref, n_ref, m_ref, 1)


def _mlstm(mq, mk, mv, gates, gate_bias, seq_chunks):
    n_tok = mq.shape[0]
    nchunk = n_tok // M_CHUNK
    fwd = lambda w: pl.BlockSpec((M_CHUNK, w), lambda j: (j, 0))
    bwd = lambda w: pl.BlockSpec((M_CHUNK, w), lambda j: (nchunk - 1 - j, 0))
    kern = functools.partial(_mlstm_kernel, seq_chunks=seq_chunks)
    return pl.pallas_call(
        kern, grid=(nchunk,),
        out_shape=(jax.ShapeDtypeStruct((n_tok, M_WIDTH), F32), jax.ShapeDtypeStruct((n_tok, M_WIDTH), F32)),
        in_specs=[fwd(M_WIDTH), fwd(M_WIDTH), fwd(M_WIDTH), fwd(LANES),
                  bwd(M_WIDTH), bwd(M_WIDTH), bwd(M_WIDTH), bwd(LANES),
                  pl.BlockSpec((1, LANES), lambda j: (0, 0))],
        out_specs=(fwd(M_WIDTH), bwd(M_WIDTH)),
        scratch_shapes=[pltpu.VMEM((2 * M_HEADS, M_HEAD_DIM, M_HEAD_DIM), F32),
                        pltpu.VMEM((2 * M_HEADS, LANES), F32),
                        pltpu.VMEM((2 * M_HEADS, LANES), F32)],
        name="mlstm",
        compiler_params=pltpu.CompilerParams(dimension_semantics=("arbitrary",), vmem_limit_bytes=VMEM_LIMIT),
    )(mq, mk, mv, gates, mq, mk, mv, gates, gate_bias)


def _seq_block_scalars(j, seq_blocks):
    start = jnp.int32(0)
    count = jnp.int32(seq_blocks[0][1] - seq_blocks[0][0])
    for s, e in seq_blocks[1:]:
        inside = j >= s
        start = jnp.where(inside, s, start)
        count = jnp.where(inside, e - s, count)
    return j - start, count


def _natten_kernel(q_ref, kp_ref, kc_ref, kn_ref, vp_ref, vc_ref, vn_ref, bias_ref, o_ref, kcat, vcat, *, seq_blocks):
    j = pl.program_id(1)
    jl, nb = _seq_block_scalars(j, seq_blocks)
    rows = nb * NA_ROWS_PER_STEP
    kcat[0:NA_BLOCK, :] = kp_ref[...]
    kcat[NA_BLOCK:2 * NA_BLOCK, :] = kc_ref[...]
    kcat[2 * NA_BLOCK:3 * NA_BLOCK, :] = kn_ref[...]
    vcat[0:NA_BLOCK, :] = vp_ref[...]
    vcat[NA_BLOCK:2 * NA_BLOCK, :] = vc_ref[...]
    vcat[2 * NA_BLOCK:3 * NA_BLOCK, :] = vn_ref[...]
    lane = lax.broadcasted_iota(I32, (GRID_W, LANES), 1)
    first_head = lane < N_HEAD_DIM

    def row_body(i, carry):
        r = jl * NA_ROWS_PER_STEP + i
        r0 = jnp.clip(r - WIN_H // 2, 0, rows - WIN_H)
        cls = r0 - r + (WIN_H - 1)
        ls = pl.multiple_of((r0 - jl * NA_ROWS_PER_STEP + NA_ROWS_PER_STEP) * GRID_W, GRID_W)
        qi = q_ref[pl.ds(pl.multiple_of(i * GRID_W, GRID_W), GRID_W), :]
        kband = kcat[pl.ds(ls, WIN_H * GRID_W), :]
        vband = vcat[pl.ds(ls, WIN_H * GRID_W), :]
        outs = []
        for hh in range(2):
            keep = first_head if hh == 0 else jnp.logical_not(first_head)
            qm = jnp.where(keep, qi, jnp.zeros_like(qi))
            s = _nt_dot(qm, kband) + bias_ref[hh, cls]
            mx = jnp.max(s, axis=1, keepdims=True)
            p = jnp.exp(s - mx)
            den = jnp.sum(p, axis=1, keepdims=True)
            outs.append(_dot(p.astype(BF16), vband) / den)
        o_ref[pl.ds(pl.multiple_of(i * GRID_W, GRID_W), GRID_W), :] = jnp.where(first_head, outs[0], outs[1]).astype(o_ref.dtype)
        return carry

    lax.fori_loop(0, NA_ROWS_PER_STEP, row_body, 0)


def _natten(nq, nk, nv, bias_tab, seq_blocks):
    n_tok = nq.shape[0]
    nblk = n_tok // NA_BLOCK
    npair = N_HEADS // 2

    def nbr_map(delta):
        def index_map(p, j):
            lo = jnp.int32(seq_blocks[0][0])
            hi = jnp.int32(seq_blocks[0][1] - 1)
            for s, e in seq_blocks[1:]:
                inside = j >= s
                lo = jnp.where(inside, s, lo)
                hi = jnp.where(inside, e - 1, hi)
            return (jnp.clip(j + delta, lo, hi), p)
        return index_map

    blk = lambda im: pl.BlockSpec((NA_BLOCK, LANES), im)
    cur = lambda p, j: (j, p)
    kern = functools.partial(_natten_kernel, seq_blocks=seq_blocks)
    return pl.pallas_call(
        kern, grid=(npair, nblk),
        out_shape=jax.ShapeDtypeStruct((n_tok, N_WIDTH), BF16),
        in_specs=[blk(cur), blk(nbr_map(-1)), blk(cur), blk(nbr_map(1)),
                  blk(nbr_map(-1)), blk(cur), blk(nbr_map(1)),
                  pl.BlockSpec((2, NA_CLASSES, GRID_W, WIN_H * GRID_W), lambda p, j: (p, 0, 0, 0))],
        out_specs=blk(cur),
        scratch_shapes=[pltpu.VMEM((3 * NA_BLOCK, LANES), BF16), pltpu.VMEM((3 * NA_BLOCK, LANES), BF16)],
        name="natten",
        compiler_params=pltpu.CompilerParams(dimension_semantics=("arbitrary", "arbitrary"), vmem_limit_bytes=VMEM_LIMIT),
    )(nq, nk, nk, nk, nv, nv, nv, bias_tab)


def _natten_bias_table(rpb):
    c = np.arange(GRID_W)
    c0 = np.clip(c - WIN_W // 2, 0, GRID_W - WIN_W)
    kc = np.arange(GRID_W)
    valid = (kc[None, :] >= c0[:, None]) & (kc[None, :] < c0[:, None] + WIN_W)
    dc = np.clip(kc[None, :] - c[:, None] + WIN_W - 1, 0, 2 * WIN_W - 2)
    onehot = (dc.reshape(-1)[None, :] == np.arange(2 * WIN_W - 1)[:, None]).astype(np.float32)
    e1 = jnp.einsum('hrd,dx->hrx', rpb.astype(F32), jnp.asarray(onehot), precision=lax.Precision.HIGHEST)
    e1 = e1.reshape(N_HEADS, 2 * WIN_H - 1, GRID_W, GRID_W)
    e1 = jnp.where(valid[None, None], e1, NEG_BIG)
    tab = jnp.stack([e1[:, cls:cls + WIN_H] for cls in range(NA_CLASSES)], axis=1)
    tab = tab.transpose(0, 1, 3, 2, 4)
    return tab.reshape(N_HEADS, NA_CLASSES, GRID_W, WIN_H * GRID_W)


def _mix_kernel(x_ref, hf_ref, hb_ref, mo_ref, na_ref, mg_ref, gh_ref, wa_ref, wb_ref, wo_ref, g2_ref, wq_ref, sk_ref,
                h_ref, hn_ref, st_ref):
    d = x_ref.shape[1]
    hm = hf_ref[...] + hb_ref[...]
    parts = []
    for h in range(M_HEADS):
        hs = hm[:, h * M_HEAD_DIM:(h + 1) * M_HEAD_DIM]
        parts.append(hs * lax.rsqrt(jnp.mean(hs * hs, axis=-1, keepdims=True) + EPS))
    hm = jnp.concatenate(parts, axis=1) * gh_ref[...]
    ya = _dot((jax.nn.sigmoid(mo_ref[...]) * hm).astype(BF16), wa_ref[...])
    yb = _dot(na_ref[...], wb_ref[...])
    mg = mg_ref[...]
    mixed = jax.nn.sigmoid(mg[:, 0:d]) * ya + jax.nn.sigmoid(mg[:, d:2 * d]) * yb
    hres = x_ref[...] + _dot(mixed.astype(BF16), wo_ref[...])
    h_ref[...] = hres
    hn = hres * lax.rsqrt(jnp.mean(hres * hres, axis=-1, keepdims=True) + EPS) * g2_ref[...]
    hn_ref[...] = hn
    qp = _dot(hn.astype(BF16), wq_ref[...]).astype(BF16)
    for b in range(2 * PEER_HEADS):
        lb = b // 2
        st_ref[b] = _nt_dot(sk_ref[b], qp[:, lb * LANES:(lb + 1) * LANES])


def _mix(x, hf, hb, mo, na, merge, gh, wa, wb, wo, g2, wq, skp, tm):
    n_tok, d = x.shape
    tok = lambda w: pl.BlockSpec((tm, w), lambda i: (i, 0))
    full = lambda a: pl.BlockSpec(a.shape, lambda i: (0,) * a.ndim)
    nsk = 2 * PEER_HEADS
    return pl.pallas_call(
        _mix_kernel, grid=(n_tok // tm,),
        out_shape=(jax.ShapeDtypeStruct((n_tok, d), F32), jax.ShapeDtypeStruct((n_tok, d), F32),
                   jax.ShapeDtypeStruct((nsk, N_SUBKEYS, n_tok), F32)),
        in_specs=[tok(d), tok(M_WIDTH), tok(M_WIDTH), tok(M_WIDTH), tok(N_WIDTH), tok(2 * d),
                  full(gh), full(wa), full(wb), full(wo), full(g2), full(wq), full(skp)],
        out_specs=(tok(d), tok(d), pl.BlockSpec((nsk, N_SUBKEYS, tm), lambda i: (0, 0, i))),
        name="mix",
        compiler_params=pltpu.CompilerParams(dimension_semantics=("arbitrary",), vmem_limit_bytes=VMEM_LIMIT),
    )(x, hf, hb, mo, na, merge, gh, wa, wb, wo, g2, wq, skp)


CAND_ROWS = PEER_TOPK + 7 * 8 + 8


def _cand_constants(tt):
    ii = np.concatenate([np.zeros(16), np.repeat(np.arange(1, 8), 8), np.arange(8, 16)]).astype(np.int64)
    jj = np.concatenate([np.arange(16), np.tile(np.arange(8), 7), np.zeros(8)]).astype(np.int64)
    valid = (ii + 1) * (jj + 1) <= PEER_TOPK
    pos = (ii * PEER_TOPK + jj).astype(np.int32)
    mask = np.where(valid, 0.0, -np.inf).astype(np.float32)
    return (jnp.asarray(np.broadcast_to(pos[:, None], (CAND_ROWS, tt))),
            jnp.asarray(np.broadcast_to(mask[:, None], (CAND_ROWS, tt))))


def _topk_kernel(s_ref, pos_ref, mask_ref, idx_ref, g_ref, va, ia, vb, ib, ids, gts):
    tt = s_ref.shape[2]
    iota = lax.broadcasted_iota(I32, (N_SUBKEYS, tt), 0)
    cpos = pos_ref[...]
    cmask = mask_ref[...]

    def head_body(h, carry):
        for half, (vref, iref) in enumerate(((va, ia), (vb, ib))):
            s = s_ref[2 * h + half]
            for kk in range(PEER_TOPK):
                m = jnp.max(s, axis=0, keepdims=True)
                pos = jnp.min(jnp.where(s == m, iota, N_SUBKEYS), axis=0, keepdims=True)
                vref[kk:kk + 1, :] = m
                iref[kk:kk + 1, :] = pos
                s = jnp.where(iota == pos, -jnp.inf, s)
        a_lo, a_hi = va[0:8, :], va[8:16, :]
        ia_lo, ia_hi = ia[0:8, :], ia[8:16, :]
        b16, b8 = vb[...], vb[0:8, :]
        ib16, ib8 = ib[...], ib[0:8, :]
        cand = [a_lo[0:1, :] + b16]
        cid = [ia_lo[0:1, :] * N_SUBKEYS + ib16]
        for i in range(1, 8):
            cand.append(a_lo[i:i + 1, :] + b8)
            cid.append(ia_lo[i:i + 1, :] * N_SUBKEYS + ib8)
        cand.append(a_hi + b8[0:1, :])
        cid.append(ia_hi * N_SUBKEYS + ib8[0:1, :])
        cand = jnp.concatenate(cand, axis=0) + cmask
        cid = jnp.concatenate(cid, axis=0)
        best = []
        row0 = pl.multiple_of(h * PEER_TOPK, PEER_TOPK)
        for kk in range(PEER_TOPK):
            m = jnp.max(cand, axis=0, keepdims=True)
            pos = jnp.min(jnp.where(cand == m, cpos, PEER_TOPK * PEER_TOPK), axis=0, keepdims=True)
            sel = cpos == pos
            best.append(m)
            ids[pl.ds(row0 + kk, 1), :] = jnp.max(jnp.where(sel, cid, -1), axis=0, keepdims=True)
            cand = jnp.where(sel, -jnp.inf, cand)
        best = jnp.concatenate(best, axis=0)
        ex = jnp.exp(best - best[0:1, :])
        gts[pl.ds(row0, PEER_TOPK), :] = ex / jnp.sum(ex, axis=0, keepdims=True)
        return carry

    lax.fori_loop(0, PEER_HEADS, head_body, 0)
    idx_ref[...] = ids[...].T
    g_ref[...] = gts[...].T


def _topk(scores_t, tt):
    nsk, _, n_tok = scores_t.shape
    cpos, cmask = _cand_constants(tt)
    const = pl.BlockSpec((CAND_ROWS, tt), lambda i: (0, 0))
    return pl.pallas_call(
        _topk_kernel, grid=(n_tok // tt,),
        out_shape=(jax.ShapeDtypeStruct((n_tok, PEER_SLOTS), I32), jax.ShapeDtypeStruct((n_tok, PEER_SLOTS), F32)),
        in_specs=[pl.BlockSpec((nsk, N_SUBKEYS, tt), lambda i: (0, 0, i)), const, const],
        out_specs=(pl.BlockSpec((tt, PEER_SLOTS), lambda i: (i, 0)), pl.BlockSpec((tt, PEER_SLOTS), lambda i: (i, 0))),
        scratch_shapes=[pltpu.VMEM((PEER_TOPK, tt), F32), pltpu.VMEM((PEER_TOPK, tt), I32),
                        pltpu.VMEM((PEER_TOPK, tt), F32), pltpu.VMEM((PEER_TOPK, tt), I32),
                        pltpu.VMEM((PEER_SLOTS, tt), I32), pltpu.VMEM((PEER_SLOTS, tt), F32)],
        name="topk",
        compiler_params=pltpu.CompilerParams(dimension_semantics=("arbitrary",), vmem_limit_bytes=VMEM_LIMIT),
    )(scores_t, cpos, cmask)


EXPERT_TOKENS = 8


def _experts_kernel(idx_ref, hn_ref, h_ref, g_ref, uv_hbm, o_ref, buf, sem):
    s = pl.program_id(0)
    ns = pl.num_programs(0)
    d = hn_ref.shape[1]
    nc = d // LANES
    group = PEER_SLOTS // (2 * nc)

    def issue(dst_slot, t, piece):
        for kk in range(piece * group, (piece + 1) * group):
            r = t * PEER_SLOTS + kk
            pltpu.make_async_copy(uv_hbm.at[idx_ref[t, kk]], buf.at[dst_slot, :, pl.ds(r, 1), :],
                                  sem.at[dst_slot]).start(priority=kk % 2)

    def step(do_issue, do_compute):
        dst_slot = lax.rem(s, 2)
        slot = 1 - dst_slot
        if do_compute:
            pltpu.make_async_copy(buf.at[slot], buf.at[slot], sem.at[slot]).wait()
            gt = g_ref[...].T
        ys = []
        for t in range(EXPERT_TOKENS):
            rs = slice(t * PEER_SLOTS, (t + 1) * PEER_SLOTS)
            acc = None
            for c in range(nc):
                if do_issue:
                    issue(dst_slot, t, c)
                if do_compute:
                    term = buf[slot, c, rs, :] * hn_ref[t:t + 1, c * LANES:(c + 1) * LANES]
                    acc = term if acc is None else acc + term
            if do_compute:
                a = jnp.sum(acc, axis=1, keepdims=True)
                hcoef = jax.nn.gelu(a) * gt[:, t:t + 1]
            yt = []
            for c in range(nc):
                if do_issue:
                    issue(dst_slot, t, nc + c)
                if do_compute:
                    yt.append(jnp.sum(hcoef * buf[slot, nc + c, rs, :], axis=0, keepdims=True))
            if do_compute:
                ys.append(jnp.concatenate(yt, axis=1))
        if do_compute:
            o_ref[...] = h_ref[...] + jnp.concatenate(ys, axis=0)

    @pl.when(s == 0)
    def _():
        step(True, False)

    @pl.when(jnp.logical_and(s > 0, s < ns - 1))
    def _():
        step(True, True)

    @pl.when(s == ns - 1)
    def _():
        step(False, True)


def _experts(idx, hn, hres, gates, uv):
    n_tok, d = hn.shape
    nstep = n_tok // EXPERT_TOKENS
    cur = lambda s: (jnp.minimum(s, nstep - 1), 0)
    prev = lambda s: (jnp.maximum(s - 1, 0), 0)
    tokp = pl.BlockSpec((EXPERT_TOKENS, d), prev)
    return pl.pallas_call(
        _experts_kernel, grid=(nstep + 1,),
        out_shape=jax.ShapeDtypeStruct((n_tok, d), F32),
        in_specs=[pl.BlockSpec((EXPERT_TOKENS, PEER_SLOTS), cur, memory_space=pltpu.SMEM),
                  tokp, tokp, pl.BlockSpec((EXPERT_TOKENS, PEER_SLOTS), prev),
                  pl.BlockSpec(memory_space=pl.ANY)],
        out_specs=tokp,
        scratch_shapes=[pltpu.VMEM((2, 2 * d // LANES, EXPERT_TOKENS * PEER_SLOTS, LANES), F32),
                        pltpu.SemaphoreType.DMA((2,))],
        name="experts",
        compiler_params=pltpu.CompilerParams(dimension_semantics=("arbitrary",), vmem_limit_bytes=VMEM_LIMIT),
    )(idx, hn, hres, gates, uv)


def _layer(x, seqs, norm1_g, w_in, b_igate, b_fgate, mlstm_norm_g, qk_norm_q, qk_norm_k, rpb,
           w_branch_a, w_branch_b, w_out, norm2_g, peer_wq, peer_subkeys, peer_u, peer_v):
    n_tok, d = x.shape
    tm = 256
    row = lambda a: a.astype(F32).reshape(1, -1)

    c0, c1, c2, c3 = 3 * M_WIDTH, 4 * M_WIDTH, 4 * M_WIDTH + 4 * M_HEADS, 4 * M_WIDTH + 4 * M_HEADS + 3 * N_WIDTH
    wm = w_in[:, 0:c1].astype(BF16)
    wg = jnp.pad(w_in[:, c1:c2], ((0, 0), (0, LANES - 4 * M_HEADS))).astype(BF16)
    wn = w_in[:, c2:c3].astype(BF16)
    wmg = w_in[:, c3:].astype(BF16)
    head_of = np.arange(N_WIDTH) // N_HEAD_DIM
    bd = jnp.asarray((head_of[:, None] == head_of[None, :]).astype(np.float32) / N_HEAD_DIM, BF16)
    gq = jnp.tile(qk_norm_q.astype(F32), N_HEADS).reshape(1, N_WIDTH)
    gk = jnp.tile(qk_norm_k.astype(F32), N_HEADS).reshape(1, N_WIDTH)
    gate_bias = jnp.pad(jnp.concatenate([b_igate.reshape(-1), b_fgate.reshape(-1)]).astype(F32),
                        (0, LANES - 4 * M_HEADS)).reshape(1, LANES)
    bias_tab = _natten_bias_table(rpb)
    half = PEER_KEY_DIM // 2
    sk = peer_subkeys.astype(F32).reshape(2 * PEER_HEADS, N_SUBKEYS, half)
    z = jnp.zeros_like(sk)
    odd = (np.arange(2 * PEER_HEADS) % 2 == 1)[:, None, None]
    skp = jnp.where(odd, jnp.concatenate([z, sk], axis=-1), jnp.concatenate([sk, z], axis=-1)).astype(BF16)
    uv = jnp.concatenate([peer_u, peer_v], axis=1).reshape(peer_u.shape[0], 2 * d // LANES, 1, LANES)

    seq_chunks = tuple((s // M_CHUNK, (s + t) // M_CHUNK) for s, t in seqs)
    seq_blocks = tuple((s // NA_BLOCK, (s + t) // NA_BLOCK) for s, t in seqs)

    mq, mk, mv, mo, gates, nq, nk, nv, merge = _proj(x, row(norm1_g), wm, wg, wn, wmg, bd, gq, gk, tm)
    hf, hb = _mlstm(mq, mk, mv, gates, gate_bias, seq_chunks)
    na = _natten(nq, nk, nv, bias_tab, seq_blocks)
    hres, hn, scores_t = _mix(x, hf, hb, mo, na, merge, row(mlstm_norm_g), w_branch_a.astype(BF16),
                              w_branch_b.astype(BF16), w_out.astype(BF16), row(norm2_g), peer_wq.astype(BF16), skp, tm)
    idx, pg = _topk(scores_t, LANES)
    return _experts(idx, hn, hres, pg, uv)


def kernel(x_prompt, x_sample, norm1_g, w_in, b_igate, b_fgate, mlstm_norm_g, qk_norm_q, qk_norm_k, rpb,
           w_branch_a, w_branch_b, w_out, norm2_g, peer_wq, peer_subkeys, peer_u, peer_v):
    depth = w_in.shape[0]
    d = x_prompt.shape[-1]
    groups = (x_prompt, x_sample)
    seqs, start = [], 0
    for gx in groups:
        for _ in range(gx.shape[0]):
            seqs.append((start, gx.shape[1]))
            start += gx.shape[1]
    for s, t in seqs:
        assert t % NA_BLOCK == 0 and t % M_CHUNK == 0 and t // GRID_W >= WIN_H
    x = jnp.concatenate([gx.reshape(-1, d) for gx in groups], axis=0)
    for l in range(depth):
        x = _layer(x, tuple(seqs), norm1_g[l], w_in[l], b_igate[l], b_fgate[l], mlstm_norm_g[l], qk_norm_q[l],
                   qk_norm_k[l], rpb[l], w_branch_a[l], w_branch_b[l], w_out[l], norm2_g[l], peer_wq[l],
                   peer_subkeys[l], peer_u[l], peer_v[l])
    outs, start = [], 0
    for gx in groups:
        cnt = gx.shape[0] * gx.shape[1]
        outs.append(x[start:start + cnt].reshape(gx.shape))
        start += cnt
    return tuple(outs)
```

```python
import functools

import numpy as np
import jax
import jax.numpy as jnp
from jax import lax
from jax.experimental import pallas as pl
from jax.experimental.pallas import tpu as pltpu

F32 = jnp.float32
BF16 = jnp.bfloat16
I32 = jnp.int32

EPS = 1e-6
GRID_W = 64
M_HEADS = 4
M_HEAD_DIM = 128
M_WIDTH = M_HEADS * M_HEAD_DIM
M_CHUNK = 128
N_HEADS = 8
N_HEAD_DIM = 64
N_WIDTH = N_HEADS * N_HEAD_DIM
WIN_H = 8
WIN_W = 16
PEER_HEADS = 8
N_SUBKEYS = 128
PEER_KEY_DIM = 128
PEER_TOPK = 16
PEER_SLOTS = PEER_HEADS * PEER_TOPK

LANES = 128
VMEM_LIMIT = 48 * 1024 * 1024

NA_ROWS_PER_STEP = 8
NA_BLOCK = NA_ROWS_PER_STEP * GRID_W
NA_CLASSES = WIN_H
NEG_BIG = -1e30


def _nt_dot(a, b):
    return lax.dot_general(a, b, (((1,), (1,)), ((), ())), preferred_element_type=F32)


def _dot(a, b):
    return jnp.dot(a, b, preferred_element_type=F32)


def _split_bf16(a):
    hi = a.astype(BF16)
    lo = (a - hi.astype(F32)).astype(BF16)
    return hi, lo


def _proj_kernel(x_ref, g1_ref, wm_ref, wg_ref, wn_ref, wmg_ref, bd_ref, gq_ref, gk_ref,
                 mq_ref, mk_ref, mv_ref, mo_ref, gates_ref, nq_ref, nk_ref, nv_ref, merge_ref):
    x = x_ref[...]
    ms = jnp.mean(x * x, axis=-1, keepdims=True)
    xn = (x * lax.rsqrt(ms + EPS) * g1_ref[...]).astype(BF16)

    m = _dot(xn, wm_ref[...])
    mq_ref[...] = (m[:, 0:M_WIDTH] * (M_HEAD_DIM ** -0.5)).astype(BF16)
    mk_ref[...] = m[:, M_WIDTH:2 * M_WIDTH].astype(BF16)
    mv_ref[...] = m[:, 2 * M_WIDTH:3 * M_WIDTH].astype(BF16)
    mo_ref[...] = m[:, 3 * M_WIDTH:4 * M_WIDTH]
    gates_ref[...] = _dot(xn, wg_ref[...])

    n = _dot(xn, wn_ref[...])
    bd = bd_ref[...]

    def head_norm(t, g):
        hi, lo = _split_bf16(t * t)
        msq = _dot(hi, bd) + _dot(lo, bd)
        return t * lax.rsqrt(msq + EPS) * g

    nq_ref[...] = (head_norm(n[:, 0:N_WIDTH], gq_ref[...]) * (N_HEAD_DIM ** -0.5)).astype(BF16)
    nk_ref[...] = head_norm(n[:, N_WIDTH:2 * N_WIDTH], gk_ref[...]).astype(BF16)
    nv_ref[...] = n[:, 2 * N_WIDTH:3 * N_WIDTH].astype(BF16)
    merge_ref[...] = _dot(xn, wmg_ref[...])


def _proj(x, g1, wm, wg, wn, wmg, bd, gq, gk, tm):
    n_tok, d = x.shape
    grid = (n_tok // tm,)
    tok = lambda w: pl.BlockSpec((tm, w), lambda i: (i, 0))
    full = lambda a: pl.BlockSpec(a.shape, lambda i: (0,) * a.ndim)
    out_shape = (
        jax.ShapeDtypeStruct((n_tok, M_WIDTH), BF16),
        jax.ShapeDtypeStruct((n_tok, M_WIDTH), BF16),
        jax.ShapeDtypeStruct((n_tok, M_WIDTH), BF16),
        jax.ShapeDtypeStruct((n_tok, M_WIDTH), F32),
        jax.ShapeDtypeStruct((n_tok, LANES), F32),
        jax.ShapeDtypeStruct((n_tok, N_WIDTH), BF16),
        jax.ShapeDtypeStruct((n_tok, N_WIDTH), BF16),
        jax.ShapeDtypeStruct((n_tok, N_WIDTH), BF16),
        jax.ShapeDtypeStruct((n_tok, 2 * d), F32),
    )
    out_specs = (tok(M_WIDTH), tok(M_WIDTH), tok(M_WIDTH), tok(M_WIDTH), tok(LANES),
                 tok(N_WIDTH), tok(N_WIDTH), tok(N_WIDTH), tok(2 * d))
    return pl.pallas_call(
        _proj_kernel, grid=grid, out_shape=out_shape,
        in_specs=[tok(d), full(g1), full(wm), full(wg), full(wn), full(wmg), full(bd), full(gq), full(gk)],
        out_specs=out_specs, name="proj",
        compiler_params=pltpu.CompilerParams(dimension_semantics=("arbitrary",), vmem_limit_bytes=VMEM_LIMIT),
    )(x, g1, wm, wg, wn, wmg, bd, gq, gk)


def _log_sigmoid(z):
    return jnp.minimum(z, 0.0) - jnp.log1p(jnp.exp(-jnp.abs(z)))


def _mlstm_direction(q_ref, k_ref, v_ref, g_ref, bias_ref, out_ref, c_ref, n_ref, m_ref, direction):
    L = M_CHUNK
    g = g_ref[...] + bias_ref[...]
    lane = lax.broadcasted_iota(I32, (L, LANES), 1)
    lg = jnp.where(lane >= 2 * M_HEADS, _log_sigmoid(g), g)
    t_idx = lax.broadcasted_iota(I32, (L, L), 0)
    s_idx = lax.broadcasted_iota(I32, (L, L), 1)
    causal = (s_idx <= t_idx) if direction == 0 else (s_idx >= t_idx)
    tri = causal.astype(BF16)
    hi, lo = _split_bf16(lg)
    bc = _dot(tri, hi) + _dot(tri, lo)
    bt = bc.T
    lgt = lg.T
    for h in range(M_HEADS):
        u = direction * M_HEADS + h
        cf = 2 * M_HEADS + direction * M_HEADS + h
        ci = direction * M_HEADS + h
        bcol = bc[:, cf:cf + 1]
        brow = bt[cf:cf + 1, :]
        li_row = lgt[ci:ci + 1, :]
        li_col = lg[:, ci:ci + 1]
        btot = brow[:, L - 1:L] if direction == 0 else brow[:, 0:1]
        hs = slice(h * M_HEAD_DIM, (h + 1) * M_HEAD_DIM)
        q = q_ref[:, hs]
        k = k_ref[:, hs]
        v = v_ref[:, hs]
        m_prev = m_ref[u:u + 1, 0:1]
        c_prev = c_ref[u]
        n_prev = n_ref[u:u + 1, :]

        d = jnp.where(causal, bcol - brow + li_row, -jnp.inf)
        a_inter = bcol + m_prev
        m_t = jnp.maximum(a_inter, jnp.max(d, axis=1, keepdims=True))
        p = _nt_dot(q, k) * jnp.exp(d - m_t)
        w_inter = jnp.exp(a_inter - m_t)
        num = _dot(p.astype(BF16), v) + w_inter * _nt_dot(q, c_prev.astype(BF16))
        den = jnp.sum(p, axis=1, keepdims=True) + w_inter * jnp.sum(q.astype(F32) * n_prev, axis=1, keepdims=True)
        out_ref[:, hs] = num / jnp.maximum(jnp.abs(den), jnp.exp(-m_t))

        w_end = btot - bcol + li_col
        m_loc = jnp.max(w_end, axis=0, keepdims=True)
        m_new = jnp.maximum(btot + m_prev, m_loc)
        decay = jnp.exp(btot + m_prev - m_new)
        e = jnp.exp(w_end - m_new)
        ev = (e * v.astype(F32)).T.astype(BF16)
        c_ref[u] = decay * c_prev + _dot(ev, k)
        n_ref[u:u + 1, :] = decay * n_prev + jnp.sum(e * k.astype(F32), axis=0, keepdims=True)
        m_ref[u:u + 1, :] = jnp.broadcast_to(m_new, (1, LANES))


def _mlstm_kernel(qf, kf, vf, gf, qb, kb, vb, gb, bias_ref, hf_ref, hb_ref, c_ref, n_ref, m_ref, *, seq_chunks):
    j = pl.program_id(0)
    jb = pl.num_programs(0) - 1 - j
    start_f = functools.reduce(jnp.logical_or, [j == s for s, _ in seq_chunks])
    start_b = functools.reduce(jnp.logical_or, [jb == e - 1 for _, e in seq_chunks])

    def reset(lo):
        c_ref[lo:lo + M_HEADS] = jnp.zeros((M_HEADS, M_HEAD_DIM, M_HEAD_DIM), F32)
        n_ref[lo:lo + M_HEADS, :] = jnp.zeros((M_HEADS, LANES), F32)
        m_ref[lo:lo + M_HEADS, :] = jnp.zeros((M_HEADS, LANES), F32)

    @pl.when(start_f)
    def _():
        reset(0)

    @pl.when(start_b)
    def _():
        reset(M_HEADS)

    _mlstm_direction(qf, kf, vf, gf, bias_ref, hf_ref, c_ref, n_ref, m_ref, 0)
    _mlstm_direction(qb, kb, vb, gb, bias_ref, hb_ref, c_ref, n_ref, m_ref, 1)


def _mlstm(mq, mk, mv, gates, gate_bias, seq_chunks):
    n_tok = mq.shape[0]
    nchunk = n_tok // M_CHUNK
    fwd = lambda w: pl.BlockSpec((M_CHUNK, w), lambda j: (j, 0))
    bwd = lambda w: pl.BlockSpec((M_CHUNK, w), lambda j: (nchunk - 1 - j, 0))
    kern = functools.partial(_mlstm_kernel, seq_chunks=seq_chunks)
    return pl.pallas_call(
        kern, grid=(nchunk,),
        out_shape=(jax.ShapeDtypeStruct((n_tok, M_WIDTH), F32), jax.ShapeDtypeStruct((n_tok, M_WIDTH), F32)),
        in_specs=[fwd(M_WIDTH), fwd(M_WIDTH), fwd(M_WIDTH), fwd(LANES),
                  bwd(M_WIDTH), bwd(M_WIDTH), bwd(M_WIDTH), bwd(LANES),
                  pl.BlockSpec((1, LANES), lambda j: (0, 0))],
        out_specs=(fwd(M_WIDTH), bwd(M_WIDTH)),
        scratch_shapes=[pltpu.VMEM((2 * M_HEADS, M_HEAD_DIM, M_HEAD_DIM), F32),
                        pltpu.VMEM((2 * M_HEADS, LANES), F32),
                        pltpu.VMEM((2 * M_HEADS, LANES), F32)],
        name="mlstm",
        compiler_params=pltpu.CompilerParams(dimension_semantics=("arbitrary",), vmem_limit_bytes=VMEM_LIMIT),
    )(mq, mk, mv, gates, mq, mk, mv, gates, gate_bias)


def _seq_block_scalars(j, seq_blocks):
    start = jnp.int32(0)
    count = jnp.int32(seq_blocks[0][1] - seq_blocks[0][0])
    for s, e in seq_blocks[1:]:
        inside = j >= s
        start = jnp.where(inside, s, start)
        count = jnp.where(inside, e - s, count)
    return j - start, count


def _natten_kernel(q_ref, kp_ref, kc_ref, kn_ref, vp_ref, vc_ref, vn_ref, bias_ref, o_ref, kcat, vcat, *, seq_blocks):
    j = pl.program_id(1)
    jl, nb = _seq_block_scalars(j, seq_blocks)
    rows = nb * NA_ROWS_PER_STEP
    kcat[0:NA_BLOCK, :] = kp_ref[...]
    kcat[NA_BLOCK:2 * NA_BLOCK, :] = kc_ref[...]
    kcat[2 * NA_BLOCK:3 * NA_BLOCK, :] = kn_ref[...]
    vcat[0:NA_BLOCK, :] = vp_ref[...]
    vcat[NA_BLOCK:2 * NA_BLOCK, :] = vc_ref[...]
    vcat[2 * NA_BLOCK:3 * NA_BLOCK, :] = vn_ref[...]
    lane = lax.broadcasted_iota(I32, (GRID_W, LANES), 1)
    first_head = lane < N_HEAD_DIM

    def row_body(i, carry):
        r = jl * NA_ROWS_PER_STEP + i
        r0 = jnp.clip(r - WIN_H // 2, 0, rows - WIN_H)
        cls = r0 - r + (WIN_H - 1)
        ls = pl.multiple_of((r0 - jl * NA_ROWS_PER_STEP + NA_ROWS_PER_STEP) * GRID_W, GRID_W)
        qi = q_ref[pl.ds(pl.multiple_of(i * GRID_W, GRID_W), GRID_W), :]
        kband = kcat[pl.ds(ls, WIN_H * GRID_W), :]
        vband = vcat[pl.ds(ls, WIN_H * GRID_W), :]
        zero = jnp.zeros_like(qi)
        q2 = jnp.concatenate([jnp.where(first_head, qi, zero), jnp.where(first_head, zero, qi)], axis=0)
        s = _nt_dot(q2, kband) + bias_ref[cls]
        mx = jnp.max(s, axis=1, keepdims=True)
        p = jnp.exp(s - mx)
        den = jnp.sum(p, axis=1, keepdims=True)
        o2 = _dot(p.astype(BF16), vband) / den
        o_ref[pl.ds(pl.multiple_of(i * GRID_W, GRID_W), GRID_W), :] = jnp.where(
            first_head, o2[0:GRID_W], o2[GRID_W:2 * GRID_W]).astype(o_ref.dtype)
        return carry

    lax.fori_loop(0, NA_ROWS_PER_STEP, row_body, 0, unroll=True)


def _natten(nq, nk, nv, bias_tab, seq_blocks):
    n_tok = nq.shape[0]
    nblk = n_tok // NA_BLOCK
    npair = N_HEADS // 2

    def nbr_map(delta):
        def index_map(p, j):
            lo = jnp.int32(seq_blocks[0][0])
            hi = jnp.int32(seq_blocks[0][1] - 1)
            for s, e in seq_blocks[1:]:
                inside = j >= s
                lo = jnp.where(inside, s, lo)
                hi = jnp.where(inside, e - 1, hi)
            return (jnp.clip(j + delta, lo, hi), p)
        return index_map

    blk = lambda im: pl.BlockSpec((NA_BLOCK, LANES), im)
    cur = lambda p, j: (j, p)
    kern = functools.partial(_natten_kernel, seq_blocks=seq_blocks)
    return pl.pallas_call(
        kern, grid=(npair, nblk),
        out_shape=jax.ShapeDtypeStruct((n_tok, N_WIDTH), BF16),
        in_specs=[blk(cur), blk(nbr_map(-1)), blk(cur), blk(nbr_map(1)),
                  blk(nbr_map(-1)), blk(cur), blk(nbr_map(1)),
                  pl.BlockSpec((None, NA_CLASSES, 2 * GRID_W, WIN_H * GRID_W), lambda p, j: (p, 0, 0, 0))],
        out_specs=blk(cur),
        scratch_shapes=[pltpu.VMEM((3 * NA_BLOCK, LANES), BF16), pltpu.VMEM((3 * NA_BLOCK, LANES), BF16)],
        name="natten",
        compiler_params=pltpu.CompilerParams(dimension_semantics=("arbitrary", "arbitrary"), vmem_limit_bytes=VMEM_LIMIT),
    )(nq, nk, nk, nk, nv, nv, nv, bias_tab)


def _natten_bias_table(rpb):
    c = np.arange(GRID_W)
    c0 = np.clip(c - WIN_W // 2, 0, GRID_W - WIN_W)
    kc = np.arange(GRID_W)
    valid = (kc[None, :] >= c0[:, None]) & (kc[None, :] < c0[:, None] + WIN_W)
    dc = np.clip(kc[None, :] - c[:, None] + WIN_W - 1, 0, 2 * WIN_W - 2)
    onehot = (dc.reshape(-1)[None, :] == np.arange(2 * WIN_W - 1)[:, None]).astype(np.float32)
    e1 = jnp.einsum('hrd,dx->hrx', rpb.astype(F32), jnp.asarray(onehot), precision=lax.Precision.HIGHEST)
    e1 = e1.reshape(N_HEADS, 2 * WIN_H - 1, GRID_W, GRID_W)
    e1 = jnp.where(valid[None, None], e1, NEG_BIG)
    tab = jnp.stack([e1[:, cls:cls + WIN_H] for cls in range(NA_CLASSES)], axis=1)
    tab = tab.transpose(0, 1, 3, 2, 4)
    tab = tab.reshape(N_HEADS // 2, 2, NA_CLASSES, GRID_W, WIN_H * GRID_W).transpose(0, 2, 1, 3, 4)
    return tab.reshape(N_HEADS // 2, NA_CLASSES, 2 * GRID_W, WIN_H * GRID_W)


def _mix_kernel(x_ref, hf_ref, hb_ref, mo_ref, na_ref, mg_ref, gh_ref, wa_ref, wb_ref, wo_ref, g2_ref, wq_ref, sk_ref,
                h_ref, hn_ref, st_ref):
    d = x_ref.shape[1]
    hm = hf_ref[...] + hb_ref[...]
    parts = []
    for h in range(M_HEADS):
        hs = hm[:, h * M_HEAD_DIM:(h + 1) * M_HEAD_DIM]
        parts.append(hs * lax.rsqrt(jnp.mean(hs * hs, axis=-1, keepdims=True) + EPS))
    hm = jnp.concatenate(parts, axis=1) * gh_ref[...]
    ya = _dot((jax.nn.sigmoid(mo_ref[...]) * hm).astype(BF16), wa_ref[...])
    yb = _dot(na_ref[...], wb_ref[...])
    mg = mg_ref[...]
    mixed = jax.nn.sigmoid(mg[:, 0:d]) * ya + jax.nn.sigmoid(mg[:, d:2 * d]) * yb
    hres = x_ref[...] + _dot(mixed.astype(BF16), wo_ref[...])
    h_ref[...] = hres
    hn = hres * lax.rsqrt(jnp.mean(hres * hres, axis=-1, keepdims=True) + EPS) * g2_ref[...]
    hn_ref[...] = hn
    qp = _dot(hn.astype(BF16), wq_ref[...]).astype(BF16)
    for b in range(2 * PEER_HEADS):
        lb = b // 2
        st_ref[b] = _nt_dot(sk_ref[b], qp[:, lb * LANES:(lb + 1) * LANES])


def _mix(x, hf, hb, mo, na, merge, gh, wa, wb, wo, g2, wq, skp, tm):
    n_tok, d = x.shape
    tok = lambda w: pl.BlockSpec((tm, w), lambda i: (i, 0))
    full = lambda a: pl.BlockSpec(a.shape, lambda i: (0,) * a.ndim)
    nsk = 2 * PEER_HEADS
    return pl.pallas_call(
        _mix_kernel, grid=(n_tok // tm,),
        out_shape=(jax.ShapeDtypeStruct((n_tok, d), F32), jax.ShapeDtypeStruct((n_tok, d), F32),
                   jax.ShapeDtypeStruct((nsk, N_SUBKEYS, n_tok), F32)),
        in_specs=[tok(d), tok(M_WIDTH), tok(M_WIDTH), tok(M_WIDTH), tok(N_WIDTH), tok(2 * d),
                  full(gh), full(wa), full(wb), full(wo), full(g2), full(wq), full(skp)],
        out_specs=(tok(d), tok(d), pl.BlockSpec((nsk, N_SUBKEYS, tm), lambda i: (0, 0, i))),
        name="mix",
        compiler_params=pltpu.CompilerParams(dimension_semantics=("arbitrary",), vmem_limit_bytes=VMEM_LIMIT),
    )(x, hf, hb, mo, na, merge, gh, wa, wb, wo, g2, wq, skp)


CAND_ROWS = PEER_TOPK + 7 * 8 + 8


def _cand_constants(tt):
    ii = np.concatenate([np.zeros(16), np.repeat(np.arange(1, 8), 8), np.arange(8, 16)]).astype(np.int64)
    jj = np.concatenate([np.arange(16), np.tile(np.arange(8), 7), np.zeros(8)]).astype(np.int64)
    valid = (ii + 1) * (jj + 1) <= PEER_TOPK
    pos = (ii * PEER_TOPK + jj).astype(np.float32)
    mask = np.where(valid, 0.0, -np.inf).astype(np.float32)
    return (jnp.asarray(np.broadcast_to(pos[:, None], (CAND_ROWS, tt))),
            jnp.asarray(np.broadcast_to(mask[:, None], (CAND_ROWS, tt))))


def _topk_kernel(s_ref, pos_ref, mask_ref, idx_ref, g_ref, va, ia, vb, ib, ids, gts):
    tt = s_ref.shape[2]
    iota = lax.broadcasted_iota(I32, (N_SUBKEYS, tt), 0).astype(F32)
    cpos = pos_ref[...]
    cmask = mask_ref[...]

    def head_body(h, carry):
        for half, (vref, iref) in enumerate(((va, ia), (vb, ib))):
            s = s_ref[2 * h + half]
            for kk in range(PEER_TOPK):
                m = jnp.max(s, axis=0, keepdims=True)
                pos = jnp.min(jnp.where(s == m, iota, float(N_SUBKEYS)), axis=0, keepdims=True)
                vref[kk:kk + 1, :] = m
                iref[kk:kk + 1, :] = pos
                s = jnp.where(iota == pos, -jnp.inf, s)
        a_lo, a_hi = va[0:8, :], va[8:16, :]
        ia_lo, ia_hi = ia[0:8, :], ia[8:16, :]
        b16, b8 = vb[...], vb[0:8, :]
        ib16, ib8 = ib[...], ib[0:8, :]
        cand = [a_lo[0:1, :] + b16]
        cid = [ia_lo[0:1, :] * N_SUBKEYS + ib16]
        for i in range(1, 8):
            cand.append(a_lo[i:i + 1, :] + b8)
            cid.append(ia_lo[i:i + 1, :] * N_SUBKEYS + ib8)
        cand.append(a_hi + b8[0:1, :])
        cid.append(ia_hi * N_SUBKEYS + ib8[0:1, :])
        cand = jnp.concatenate(cand, axis=0) + cmask
        cid = jnp.concatenate(cid, axis=0)
        best = []
        row0 = pl.multiple_of(h * PEER_TOPK, PEER_TOPK)
        for kk in range(PEER_TOPK):
            m = jnp.max(cand, axis=0, keepdims=True)
            pos = jnp.min(jnp.where(cand == m, cpos, float(PEER_TOPK * PEER_TOPK)), axis=0, keepdims=True)
            sel = cpos == pos
            best.append(m)
            ids[pl.ds(row0 + kk, 1), :] = jnp.max(jnp.where(sel, cid, -1.0), axis=0, keepdims=True)
            cand = jnp.where(sel, -jnp.inf, cand)
        best = jnp.concatenate(best, axis=0)
        ex = jnp.exp(best - best[0:1, :])
        gts[pl.ds(row0, PEER_TOPK), :] = ex / jnp.sum(ex, axis=0, keepdims=True)
        return carry

    lax.fori_loop(0, PEER_HEADS, head_body, 0)
    idx_ref[...] = ids[...].T.astype(I32)
    g_ref[...] = gts[...].T


def _topk(scores_t, tt):
    nsk, _, n_tok = scores_t.shape
    cpos, cmask = _cand_constants(tt)
    const = pl.BlockSpec((CAND_ROWS, tt), lambda i: (0, 0))
    return pl.pallas_call(
        _topk_kernel, grid=(n_tok // tt,),
        out_shape=(jax.ShapeDtypeStruct((n_tok, PEER_SLOTS), I32), jax.ShapeDtypeStruct((n_tok, PEER_SLOTS), F32)),
        in_specs=[pl.BlockSpec((nsk, N_SUBKEYS, tt), lambda i: (0, 0, i)), const, const],
        out_specs=(pl.BlockSpec((tt, PEER_SLOTS), lambda i: (i, 0)), pl.BlockSpec((tt, PEER_SLOTS), lambda i: (i, 0))),
        scratch_shapes=[pltpu.VMEM((PEER_TOPK, tt), F32), pltpu.VMEM((PEER_TOPK, tt), F32),
                        pltpu.VMEM((PEER_TOPK, tt), F32), pltpu.VMEM((PEER_TOPK, tt), F32),
                        pltpu.VMEM((PEER_SLOTS, tt), F32), pltpu.VMEM((PEER_SLOTS, tt), F32)],
        name="topk",
        compiler_params=pltpu.CompilerParams(dimension_semantics=("arbitrary",), vmem_limit_bytes=VMEM_LIMIT),
    )(scores_t, cpos, cmask)


EXPERT_TOKENS = 8
ROW_WORDS = 8


def _lo_f32(w):
    return lax.bitcast_convert_type(lax.shift_left(w, 16), F32)


def _hi_f32(w):
    return lax.bitcast_convert_type(jnp.bitwise_and(w, jnp.int32(-65536)), F32)


def _pack_rows(w):
    e, d = w.shape
    b = lax.bitcast_convert_type(w.astype(BF16), jnp.uint16).astype(jnp.uint32).reshape(e, d // (2 * LANES), 2, LANES)
    return lax.bitcast_convert_type(b[:, :, 0, :] | (b[:, :, 1, :] << 16), I32)


def _experts_kernel(idx_ref, hn_ref, h_ref, g_ref, tab_hbm, o_ref, buf0, buf1, sem, *, last_step):
    s = pl.program_id(0)
    d = hn_ref.shape[1]
    npair = d // (2 * LANES)
    group = PEER_SLOTS // (2 * npair)
    bufs = (buf0, buf1)

    def issue(dst, t, piece):
        for kk in range(piece * group, (piece + 1) * group):
            r = t * PEER_SLOTS + kk
            pltpu.make_async_copy(tab_hbm.at[idx_ref[t, kk]], bufs[dst].at[pl.ds(r * ROW_WORDS, ROW_WORDS), :],
                                  sem.at[dst]).start(priority=kk % 2)

    def step(dst, do_issue, do_compute):
        src = 1 - dst
        buf = bufs[src]
        if do_compute:
            pltpu.make_async_copy(buf, buf, sem.at[src]).wait()
            gt = g_ref[...].T
        ys = []
        for t in range(EXPERT_TOKENS):
            base = t * PEER_SLOTS * ROW_WORDS
            acc = None
            for p in range(npair):
                if do_issue:
                    issue(dst, t, p)
                if do_compute:
                    w = buf[pl.ds(base + p, PEER_SLOTS, stride=ROW_WORDS), :]
                    c0, c1 = 2 * p, 2 * p + 1
                    term = (_lo_f32(w) * hn_ref[t:t + 1, c0 * LANES:(c0 + 1) * LANES]
                            + _hi_f32(w) * hn_ref[t:t + 1, c1 * LANES:(c1 + 1) * LANES])
                    acc = term if acc is None else acc + term
            if do_compute:
                a = jnp.sum(acc, axis=1, keepdims=True)
                hcoef = jax.nn.gelu(a) * gt[:, t:t + 1]
            yt = []
            for p in range(npair):
                if do_issue:
                    issue(dst, t, npair + p)
                if do_compute:
                    w = buf[pl.ds(base + npair + p, PEER_SLOTS, stride=ROW_WORDS), :]
                    yt.append(jnp.sum(hcoef * _lo_f32(w), axis=0, keepdims=True))
                    yt.append(jnp.sum(hcoef * _hi_f32(w), axis=0, keepdims=True))
            if do_compute:
                ys.append(jnp.concatenate(yt, axis=1))
        if do_compute:
            o_ref[...] = h_ref[...] + jnp.concatenate(ys, axis=0)

    even = lax.rem(s, 2) == 0
    steady = jnp.logical_and(s > 0, s < last_step)

    @pl.when(s == 0)
    def _():
        step(0, True, False)

    @pl.when(jnp.logical_and(steady, even))
    def _():
        step(0, True, True)

    @pl.when(jnp.logical_and(steady, jnp.logical_not(even)))
    def _():
        step(1, True, True)

    @pl.when(s == last_step)
    def _():
        step(last_step % 2, False, True)


def _experts(idx, hn, hres, gates, tab):
    n_tok, d = hn.shape
    assert tab.shape[1:] == (ROW_WORDS, LANES) and d == ROW_WORDS * LANES
    nstep = n_tok // EXPERT_TOKENS
    cur = lambda s: (jnp.minimum(s, nstep - 1), 0)
    prev = lambda s: (jnp.maximum(s - 1, 0), 0)
    tokp = pl.BlockSpec((EXPERT_TOKENS, d), prev)
    return pl.pallas_call(
        functools.partial(_experts_kernel, last_step=nstep), grid=(nstep + 1,),
        out_shape=jax.ShapeDtypeStruct((n_tok, d), F32),
        in_specs=[pl.BlockSpec((EXPERT_TOKENS, PEER_SLOTS), cur, memory_space=pltpu.SMEM),
                  tokp, tokp, pl.BlockSpec((EXPERT_TOKENS, PEER_SLOTS), prev),
                  pl.BlockSpec(memory_space=pl.ANY)],
        out_specs=tokp,
        scratch_shapes=[pltpu.VMEM((EXPERT_TOKENS * PEER_SLOTS * ROW_WORDS, LANES), I32),
                        pltpu.VMEM((EXPERT_TOKENS * PEER_SLOTS * ROW_WORDS, LANES), I32),
                        pltpu.SemaphoreType.DMA((2,))],
        name="experts",
        compiler_params=pltpu.CompilerParams(dimension_semantics=("arbitrary",), vmem_limit_bytes=VMEM_LIMIT),
    )(idx, hn, hres, gates, tab)


def _layer(x, seqs, norm1_g, w_in, b_igate, b_fgate, mlstm_norm_g, qk_norm_q, qk_norm_k, rpb,
           w_branch_a, w_branch_b, w_out, norm2_g, peer_wq, peer_subkeys, peer_u, peer_v):
    n_tok, d = x.shape
    tm = 256
    row = lambda a: a.astype(F32).reshape(1, -1)

    c0, c1, c2, c3 = 3 * M_WIDTH, 4 * M_WIDTH, 4 * M_WIDTH + 4 * M_HEADS, 4 * M_WIDTH + 4 * M_HEADS + 3 * N_WIDTH
    wm = w_in[:, 0:c1].astype(BF16)
    wg = jnp.pad(w_in[:, c1:c2], ((0, 0), (0, LANES - 4 * M_HEADS))).astype(BF16)
    wn = w_in[:, c2:c3].astype(BF16)
    wmg = w_in[:, c3:].astype(BF16)
    head_of = np.arange(N_WIDTH) // N_HEAD_DIM
    bd = jnp.asarray((head_of[:, None] == head_of[None, :]).astype(np.float32) / N_HEAD_DIM, BF16)
    gq = jnp.tile(qk_norm_q.astype(F32), N_HEADS).reshape(1, N_WIDTH)
    gk = jnp.tile(qk_norm_k.astype(F32), N_HEADS).reshape(1, N_WIDTH)
    gate_bias = jnp.pad(jnp.concatenate([b_igate.reshape(-1), b_fgate.reshape(-1)]).astype(F32),
                        (0, LANES - 4 * M_HEADS)).reshape(1, LANES)
    bias_tab = _natten_bias_table(rpb)
    half = PEER_KEY_DIM // 2
    sk = peer_subkeys.astype(F32).reshape(2 * PEER_HEADS, N_SUBKEYS, half)
    z = jnp.zeros_like(sk)
    odd = (np.arange(2 * PEER_HEADS) % 2 == 1)[:, None, None]
    skp = jnp.where(odd, jnp.concatenate([z, sk], axis=-1), jnp.concatenate([sk, z], axis=-1)).astype(BF16)
    uv = jnp.concatenate([_pack_rows(peer_u), _pack_rows(peer_v)], axis=1)

    seq_chunks = tuple((s // M_CHUNK, (s + t) // M_CHUNK) for s, t in seqs)
    seq_blocks = tuple((s // NA_BLOCK, (s + t) // NA_BLOCK) for s, t in seqs)

    mq, mk, mv, mo, gates, nq, nk, nv, merge = _proj(x, row(norm1_g), wm, wg, wn, wmg, bd, gq, gk, tm)
    hf, hb = _mlstm(mq, mk, mv, gates, gate_bias, seq_chunks)
    na = _natten(nq, nk, nv, bias_tab, seq_blocks)
    hres, hn, scores_t = _mix(x, hf, hb, mo, na, merge, row(mlstm_norm_g), w_branch_a.astype(BF16),
                              w_branch_b.astype(BF16), w_out.astype(BF16), row(norm2_g), peer_wq.astype(BF16), skp, tm)
    idx, pg = _topk(scores_t, 2 * LANES)
    return _experts(idx, hn, hres, pg, uv)


def kernel(x_prompt, x_sample, norm1_g, w_in, b_igate, b_fgate, mlstm_norm_g, qk_norm_q, qk_norm_k, rpb,
           w_branch_a, w_branch_b, w_out, norm2_g, peer_wq, peer_subkeys, peer_u, peer_v):
    depth = w_in.shape[0]
    d = x_prompt.shape[-1]
    groups = (x_prompt, x_sample)
    seqs, start = [], 0
    for gx in groups:
        for _ in range(gx.shape[0]):
            seqs.append((start, gx.shape[1]))
            start += gx.shape[1]
    for s, t in seqs:
        assert t % NA_BLOCK == 0 and t % M_CHUNK == 0 and t // GRID_W >= WIN_H
    x = jnp.concatenate([gx.reshape(-1, d) for gx in groups], axis=0)
    for l in range(depth):
        x = _layer(x, tuple(seqs), norm1_g[l], w_in[l], b_igate[l], b_fgate[l], mlstm_norm_g[l], qk_norm_q[l],
                   qk_norm_k[l], rpb[l], w_branch_a[l], w_branch_b[l], w_out[l], norm2_g[l], peer_wq[l],
                   peer_subkeys[l], peer_u[l], peer_v[l])
    outs, start = [], 0
    for gx in groups:
        cnt = gx.shape[0] * gx.shape[1]
        outs.append(x[start:start + cnt].reshape(gx.shape))
        start += cnt
    return tuple(outs)
```

```python
import functools

import numpy as np
import jax
import jax.numpy as jnp
from jax import lax
from jax.experimental import pallas as pl
from jax.experimental.pallas import tpu as pltpu

F32 = jnp.float32
BF16 = jnp.bfloat16
I32 = jnp.int32

EPS = 1e-6
GRID_W = 64
M_HEADS = 4
M_HEAD_DIM = 128
M_WIDTH = M_HEADS * M_HEAD_DIM
M_CHUNK = 128
N_HEADS = 8
N_HEAD_DIM = 64
N_WIDTH = N_HEADS * N_HEAD_DIM
WIN_H = 8
WIN_W = 16
PEER_HEADS = 8
N_SUBKEYS = 128
PEER_KEY_DIM = 128
PEER_TOPK = 16
PEER_SLOTS = PEER_HEADS * PEER_TOPK

LANES = 128
VMEM_LIMIT = 48 * 1024 * 1024

NA_ROWS_PER_STEP = 8
NA_BLOCK = NA_ROWS_PER_STEP * GRID_W
NA_CLASSES = WIN_H
NEG_BIG = -1e30


def _nt_dot(a, b):
    return lax.dot_general(a, b, (((1,), (1,)), ((), ())), preferred_element_type=F32)


def _dot(a, b):
    return jnp.dot(a, b, preferred_element_type=F32)


def _split_bf16(a):
    hi = a.astype(BF16)
    lo = (a - hi.astype(F32)).astype(BF16)
    return hi, lo


def _proj_kernel(x_ref, g1_ref, wm_ref, wg_ref, wn_ref, wmg_ref, bd_ref, gq_ref, gk_ref,
                 mq_ref, mk_ref, mv_ref, mo_ref, gates_ref, nq_ref, nk_ref, nv_ref, merge_ref):
    x = x_ref[...]
    ms = jnp.mean(x * x, axis=-1, keepdims=True)
    xn = (x * lax.rsqrt(ms + EPS) * g1_ref[...]).astype(BF16)

    m = _dot(xn, wm_ref[...])
    mq_ref[...] = (m[:, 0:M_WIDTH] * (M_HEAD_DIM ** -0.5)).astype(BF16)
    mk_ref[...] = m[:, M_WIDTH:2 * M_WIDTH].astype(BF16)
    mv_ref[...] = m[:, 2 * M_WIDTH:3 * M_WIDTH].astype(BF16)
    mo_ref[...] = m[:, 3 * M_WIDTH:4 * M_WIDTH]
    gates_ref[...] = _dot(xn, wg_ref[...])

    n = _dot(xn, wn_ref[...])
    bd = bd_ref[...]

    def head_norm(t, g):
        hi, lo = _split_bf16(t * t)
        msq = _dot(hi, bd) + _dot(lo, bd)
        return t * lax.rsqrt(msq + EPS) * g

    nq_ref[...] = (head_norm(n[:, 0:N_WIDTH], gq_ref[...]) * (N_HEAD_DIM ** -0.5)).astype(BF16)
    nk_ref[...] = head_norm(n[:, N_WIDTH:2 * N_WIDTH], gk_ref[...]).astype(BF16)
    nv_ref[...] = n[:, 2 * N_WIDTH:3 * N_WIDTH].astype(BF16)
    merge_ref[...] = _dot(xn, wmg_ref[...])


def _proj(x, g1, wm, wg, wn, wmg, bd, gq, gk, tm):
    n_tok, d = x.shape
    grid = (n_tok // tm,)
    tok = lambda w: pl.BlockSpec((tm, w), lambda i: (i, 0))
    full = lambda a: pl.BlockSpec(a.shape, lambda i: (0,) * a.ndim)
    out_shape = (
        jax.ShapeDtypeStruct((n_tok, M_WIDTH), BF16),
        jax.ShapeDtypeStruct((n_tok, M_WIDTH), BF16),
        jax.ShapeDtypeStruct((n_tok, M_WIDTH), BF16),
        jax.ShapeDtypeStruct((n_tok, M_WIDTH), F32),
        jax.ShapeDtypeStruct((n_tok, LANES), F32),
        jax.ShapeDtypeStruct((n_tok, N_WIDTH), BF16),
        jax.ShapeDtypeStruct((n_tok, N_WIDTH), BF16),
        jax.ShapeDtypeStruct((n_tok, N_WIDTH), BF16),
        jax.ShapeDtypeStruct((n_tok, 2 * d), F32),
    )
    out_specs = (tok(M_WIDTH), tok(M_WIDTH), tok(M_WIDTH), tok(M_WIDTH), tok(LANES),
                 tok(N_WIDTH), tok(N_WIDTH), tok(N_WIDTH), tok(2 * d))
    return pl.pallas_call(
        _proj_kernel, grid=grid, out_shape=out_shape,
        in_specs=[tok(d), full(g1), full(wm), full(wg), full(wn), full(wmg), full(bd), full(gq), full(gk)],
        out_specs=out_specs, name="proj",
        compiler_params=pltpu.CompilerParams(dimension_semantics=("arbitrary",), vmem_limit_bytes=VMEM_LIMIT),
    )(x, g1, wm, wg, wn, wmg, bd, gq, gk)


def _log_sigmoid(z):
    return jnp.minimum(z, 0.0) - jnp.log1p(jnp.exp(-jnp.abs(z)))


def _mlstm_direction(q_ref, k_ref, v_ref, g_ref, bias_ref, out_ref, c_ref, n_ref, m_ref, direction):
    L = M_CHUNK
    g = g_ref[...] + bias_ref[...]
    lane = lax.broadcasted_iota(I32, (L, LANES), 1)
    lg = jnp.where(lane >= 2 * M_HEADS, _log_sigmoid(g), g)
    t_idx = lax.broadcasted_iota(I32, (L, L), 0)
    s_idx = lax.broadcasted_iota(I32, (L, L), 1)
    causal = (s_idx <= t_idx) if direction == 0 else (s_idx >= t_idx)
    tri = causal.astype(BF16)
    hi, lo = _split_bf16(lg)
    bc = _dot(tri, hi) + _dot(tri, lo)
    bt = bc.T
    lgt = lg.T
    for h in range(M_HEADS):
        u = direction * M_HEADS + h
        cf = 2 * M_HEADS + direction * M_HEADS + h
        ci = direction * M_HEADS + h
        bcol = bc[:, cf:cf + 1]
        brow = bt[cf:cf + 1, :]
        li_row = lgt[ci:ci + 1, :]
        li_col = lg[:, ci:ci + 1]
        btot = brow[:, L - 1:L] if direction == 0 else brow[:, 0:1]
        hs = slice(h * M_HEAD_DIM, (h + 1) * M_HEAD_DIM)
        q = q_ref[:, hs]
        k = k_ref[:, hs]
        v = v_ref[:, hs]
        m_prev = m_ref[u:u + 1, 0:1]
        c_prev = c_ref[u]
        n_prev = n_ref[u:u + 1, :]

        d = jnp.where(causal, bcol - brow + li_row, -jnp.inf)
        a_inter = bcol + m_prev
        m_t = jnp.maximum(a_inter, jnp.max(d, axis=1, keepdims=True))
        p = _nt_dot(q, k) * jnp.exp(d - m_t)
        w_inter = jnp.exp(a_inter - m_t)
        num = _dot(p.astype(BF16), v) + w_inter * _nt_dot(q, c_prev.astype(BF16))
        den = jnp.sum(p, axis=1, keepdims=True) + w_inter * jnp.sum(q.astype(F32) * n_prev, axis=1, keepdims=True)
        out_ref[:, hs] = num / jnp.maximum(jnp.abs(den), jnp.exp(-m_t))

        w_end = btot - bcol + li_col
        m_loc = jnp.max(w_end, axis=0, keepdims=True)
        m_new = jnp.maximum(btot + m_prev, m_loc)
        decay = jnp.exp(btot + m_prev - m_new)
        e = jnp.exp(w_end - m_new)
        ev = (e * v.astype(F32)).T.astype(BF16)
        c_ref[u] = decay * c_prev + _dot(ev, k)
        n_ref[u:u + 1, :] = decay * n_prev + jnp.sum(e * k.astype(F32), axis=0, keepdims=True)
        m_ref[u:u + 1, :] = jnp.broadcast_to(m_new, (1, LANES))


def _mlstm_kernel(qf, kf, vf, gf, qb, kb, vb, gb, bias_ref, hf_ref, hb_ref, c_ref, n_ref, m_ref, *, seq_chunks):
    j = pl.program_id(0)
    jb = pl.num_programs(0) - 1 - j
    start_f = functools.reduce(jnp.logical_or, [j == s for s, _ in seq_chunks])
    start_b = functools.reduce(jnp.logical_or, [jb == e - 1 for _, e in seq_chunks])

    def reset(lo):
        c_ref[lo:lo + M_HEADS] = jnp.zeros((M_HEADS, M_HEAD_DIM, M_HEAD_DIM), F32)
        n_ref[lo:lo + M_HEADS, :] = jnp.zeros((M_HEADS, LANES), F32)
        m_ref[lo:lo + M_HEADS, :] = jnp.zeros((M_HEADS, LANES), F32)

    @pl.when(start_f)
    def _():
        reset(0)

    @pl.when(start_b)
    def _():
        reset(M_HEADS)

    _mlstm_direction(qf, kf, vf, gf, bias_ref, hf_ref, c_ref, n_ref, m_ref, 0)
    _mlstm_direction(qb, kb, vb, gb, bias_ref, hb_ref, c_ref, n_ref, m_ref, 1)


def _mlstm(mq, mk, mv, gates, gate_bias, seq_chunks):
    n_tok = mq.shape[0]
    nchunk = n_tok // M_CHUNK
    fwd = lambda w: pl.BlockSpec((M_CHUNK, w), lambda j: (j, 0))
    bwd = lambda w: pl.BlockSpec((M_CHUNK, w), lambda j: (nchunk - 1 - j, 0))
    kern = functools.partial(_mlstm_kernel, seq_chunks=seq_chunks)
    return pl.pallas_call(
        kern, grid=(nchunk,),
        out_shape=(jax.ShapeDtypeStruct((n_tok, M_WIDTH), F32), jax.ShapeDtypeStruct((n_tok, M_WIDTH), F32)),
        in_specs=[fwd(M_WIDTH), fwd(M_WIDTH), fwd(M_WIDTH), fwd(LANES),
                  bwd(M_WIDTH), bwd(M_WIDTH), bwd(M_WIDTH), bwd(LANES),
                  pl.BlockSpec((1, LANES), lambda j: (0, 0))],
        out_specs=(fwd(M_WIDTH), bwd(M_WIDTH)),
        scratch_shapes=[pltpu.VMEM((2 * M_HEADS, M_HEAD_DIM, M_HEAD_DIM), F32),
                        pltpu.VMEM((2 * M_HEADS, LANES), F32),
                        pltpu.VMEM((2 * M_HEADS, LANES), F32)],
        name="mlstm",
        compiler_params=pltpu.CompilerParams(dimension_semantics=("arbitrary",), vmem_limit_bytes=VMEM_LIMIT),
    )(mq, mk, mv, gates, mq, mk, mv, gates, gate_bias)


def _seq_block_scalars(j, seq_blocks):
    start = jnp.int32(0)
    count = jnp.int32(seq_blocks[0][1] - seq_blocks[0][0])
    for s, e in seq_blocks[1:]:
        inside = j >= s
        start = jnp.where(inside, s, start)
        count = jnp.where(inside, e - s, count)
    return j - start, count


def _natten_kernel(q_ref, kp_ref, kc_ref, kn_ref, vp_ref, vc_ref, vn_ref, bias_ref, o_ref, kcat, vcat, *, seq_blocks):
    j = pl.program_id(1)
    jl, nb = _seq_block_scalars(j, seq_blocks)
    rows = nb * NA_ROWS_PER_STEP
    kcat[0:NA_BLOCK, :] = kp_ref[...]
    kcat[NA_BLOCK:2 * NA_BLOCK, :] = kc_ref[...]
    kcat[2 * NA_BLOCK:3 * NA_BLOCK, :] = kn_ref[...]
    vcat[0:NA_BLOCK, :] = vp_ref[...]
    vcat[NA_BLOCK:2 * NA_BLOCK, :] = vc_ref[...]
    vcat[2 * NA_BLOCK:3 * NA_BLOCK, :] = vn_ref[...]
    lane = lax.broadcasted_iota(I32, (GRID_W, LANES), 1)
    first_head = lane < N_HEAD_DIM

    def row_body(i, carry):
        r = jl * NA_ROWS_PER_STEP + i
        r0 = jnp.clip(r - WIN_H // 2, 0, rows - WIN_H)
        cls = r0 - r + (WIN_H - 1)
        ls = pl.multiple_of((r0 - jl * NA_ROWS_PER_STEP + NA_ROWS_PER_STEP) * GRID_W, GRID_W)
        qi = q_ref[pl.ds(pl.multiple_of(i * GRID_W, GRID_W), GRID_W), :]
        kband = kcat[pl.ds(ls, WIN_H * GRID_W), :]
        vband = vcat[pl.ds(ls, WIN_H * GRID_W), :]
        zero = jnp.zeros_like(qi)
        q2 = jnp.concatenate([jnp.where(first_head, qi, zero), jnp.where(first_head, zero, qi)], axis=0)
        s = _nt_dot(q2, kband) + bias_ref[cls]
        mx = jnp.max(s, axis=1, keepdims=True)
        p = jnp.exp(s - mx)
        den = jnp.sum(p, axis=1, keepdims=True)
        o2 = _dot(p.astype(BF16), vband) / den
        o_ref[pl.ds(pl.multiple_of(i * GRID_W, GRID_W), GRID_W), :] = jnp.where(
            first_head, o2[0:GRID_W], o2[GRID_W:2 * GRID_W]).astype(o_ref.dtype)
        return carry

    lax.fori_loop(0, NA_ROWS_PER_STEP, row_body, 0, unroll=True)


def _natten(nq, nk, nv, bias_tab, seq_blocks):
    n_tok = nq.shape[0]
    nblk = n_tok // NA_BLOCK
    npair = N_HEADS // 2

    def nbr_map(delta):
        def index_map(p, j):
            lo = jnp.int32(seq_blocks[0][0])
            hi = jnp.int32(seq_blocks[0][1] - 1)
            for s, e in seq_blocks[1:]:
                inside = j >= s
                lo = jnp.where(inside, s, lo)
                hi = jnp.where(inside, e - 1, hi)
            return (jnp.clip(j + delta, lo, hi), p)
        return index_map

    blk = lambda im: pl.BlockSpec((NA_BLOCK, LANES), im)
    cur = lambda p, j: (j, p)
    kern = functools.partial(_natten_kernel, seq_blocks=seq_blocks)
    return pl.pallas_call(
        kern, grid=(npair, nblk),
        out_shape=jax.ShapeDtypeStruct((n_tok, N_WIDTH), BF16),
        in_specs=[blk(cur), blk(nbr_map(-1)), blk(cur), blk(nbr_map(1)),
                  blk(nbr_map(-1)), blk(cur), blk(nbr_map(1)),
                  pl.BlockSpec((None, NA_CLASSES, 2 * GRID_W, WIN_H * GRID_W), lambda p, j: (p, 0, 0, 0))],
        out_specs=blk(cur),
        scratch_shapes=[pltpu.VMEM((3 * NA_BLOCK, LANES), BF16), pltpu.VMEM((3 * NA_BLOCK, LANES), BF16)],
        name="natten",
        compiler_params=pltpu.CompilerParams(dimension_semantics=("arbitrary", "arbitrary"), vmem_limit_bytes=VMEM_LIMIT),
    )(nq, nk, nk, nk, nv, nv, nv, bias_tab)


def _natten_bias_table(rpb):
    c = np.arange(GRID_W)
    c0 = np.clip(c - WIN_W // 2, 0, GRID_W - WIN_W)
    kc = np.arange(GRID_W)
    valid = (kc[None, :] >= c0[:, None]) & (kc[None, :] < c0[:, None] + WIN_W)
    dc = np.clip(kc[None, :] - c[:, None] + WIN_W - 1, 0, 2 * WIN_W - 2)
    onehot = (dc.reshape(-1)[None, :] == np.arange(2 * WIN_W - 1)[:, None]).astype(np.float32)
    e1 = jnp.einsum('hrd,dx->hrx', rpb.astype(F32), jnp.asarray(onehot), precision=lax.Precision.HIGHEST)
    e1 = e1.reshape(N_HEADS, 2 * WIN_H - 1, GRID_W, GRID_W)
    e1 = jnp.where(valid[None, None], e1, NEG_BIG)
    tab = jnp.stack([e1[:, cls:cls + WIN_H] for cls in range(NA_CLASSES)], axis=1)
    tab = tab.transpose(0, 1, 3, 2, 4)
    tab = tab.reshape(N_HEADS // 2, 2, NA_CLASSES, GRID_W, WIN_H * GRID_W).transpose(0, 2, 1, 3, 4)
    return tab.reshape(N_HEADS // 2, NA_CLASSES, 2 * GRID_W, WIN_H * GRID_W)


def _mix_kernel(x_ref, hf_ref, hb_ref, mo_ref, na_ref, mg_ref, gh_ref, wa_ref, wb_ref, wo_ref, g2_ref, wq_ref, sk_ref,
                h_ref, hn_ref, st_ref):
    d = x_ref.shape[1]
    hm = hf_ref[...] + hb_ref[...]
    parts = []
    for h in range(M_HEADS):
        hs = hm[:, h * M_HEAD_DIM:(h + 1) * M_HEAD_DIM]
        parts.append(hs * lax.rsqrt(jnp.mean(hs * hs, axis=-1, keepdims=True) + EPS))
    hm = jnp.concatenate(parts, axis=1) * gh_ref[...]
    ya = _dot((jax.nn.sigmoid(mo_ref[...]) * hm).astype(BF16), wa_ref[...])
    yb = _dot(na_ref[...], wb_ref[...])
    mg = mg_ref[...]
    mixed = jax.nn.sigmoid(mg[:, 0:d]) * ya + jax.nn.sigmoid(mg[:, d:2 * d]) * yb
    hres = x_ref[...] + _dot(mixed.astype(BF16), wo_ref[...])
    h_ref[...] = hres
    hn = hres * lax.rsqrt(jnp.mean(hres * hres, axis=-1, keepdims=True) + EPS) * g2_ref[...]
    hn_ref[...] = hn
    qp = _dot(hn.astype(BF16), wq_ref[...]).astype(BF16)
    for b in range(2 * PEER_HEADS):
        lb = b // 2
        st_ref[b] = _nt_dot(sk_ref[b], qp[:, lb * LANES:(lb + 1) * LANES])


def _mix(x, hf, hb, mo, na, merge, gh, wa, wb, wo, g2, wq, skp, tm):
    n_tok, d = x.shape
    tok = lambda w: pl.BlockSpec((tm, w), lambda i: (i, 0))
    full = lambda a: pl.BlockSpec(a.shape, lambda i: (0,) * a.ndim)
    nsk = 2 * PEER_HEADS
    return pl.pallas_call(
        _mix_kernel, grid=(n_tok // tm,),
        out_shape=(jax.ShapeDtypeStruct((n_tok, d), F32), jax.ShapeDtypeStruct((n_tok, d), F32),
                   jax.ShapeDtypeStruct((nsk, N_SUBKEYS, n_tok), F32)),
        in_specs=[tok(d), tok(M_WIDTH), tok(M_WIDTH), tok(M_WIDTH), tok(N_WIDTH), tok(2 * d),
                  full(gh), full(wa), full(wb), full(wo), full(g2), full(wq), full(skp)],
        out_specs=(tok(d), tok(d), pl.BlockSpec((nsk, N_SUBKEYS, tm), lambda i: (0, 0, i))),
        name="mix",
        compiler_params=pltpu.CompilerParams(dimension_semantics=("arbitrary",), vmem_limit_bytes=VMEM_LIMIT),
    )(x, hf, hb, mo, na, merge, gh, wa, wb, wo, g2, wq, skp)


CAND_ROWS = PEER_TOPK + 7 * 8 + 8


def _cand_constants(tt):
    ii = np.concatenate([np.zeros(16), np.repeat(np.arange(1, 8), 8), np.arange(8, 16)]).astype(np.int64)
    jj = np.concatenate([np.arange(16), np.tile(np.arange(8), 7), np.zeros(8)]).astype(np.int64)
    valid = (ii + 1) * (jj + 1) <= PEER_TOPK
    pos = (ii * PEER_TOPK + jj).astype(np.float32)
    mask = np.where(valid, 0.0, -np.inf).astype(np.float32)
    return (jnp.asarray(np.broadcast_to(pos[:, None], (CAND_ROWS, tt))),
            jnp.asarray(np.broadcast_to(mask[:, None], (CAND_ROWS, tt))))


EXPERT_TOKENS = 16
ROW_WORDS = 8


def _lo_f32(w):
    return lax.bitcast_convert_type(lax.shift_left(w, 16), F32)


def _hi_f32(w):
    return lax.bitcast_convert_type(jnp.bitwise_and(w, jnp.int32(-65536)), F32)


def _pack_rows(w):
    e, d = w.shape
    b = lax.bitcast_convert_type(w.astype(BF16), jnp.uint16).astype(jnp.uint32).reshape(e, d // (2 * LANES), 2, LANES)
    return lax.bitcast_convert_type(b[:, :, 0, :] | (b[:, :, 1, :] << 16), I32)


TOPK_TILE = 2 * LANES
STEPS_PER_TILE = TOPK_TILE // EXPERT_TOKENS


def _topk_stage1(s, vref, iref, row0, iota):
    for kk in range(PEER_TOPK):
        m = jnp.max(s, axis=0, keepdims=True)
        pos = jnp.min(jnp.where(s == m, iota, float(N_SUBKEYS)), axis=0, keepdims=True)
        vref[pl.ds(row0 + kk, 1), :] = m
        iref[pl.ds(row0 + kk, 1), :] = pos
        s = jnp.where(iota == pos, -jnp.inf, s)


def _topk_stage2(va, ia, vb, ib, row0, cpos, cmask, ids, gts):
    a_lo, a_hi = va[pl.ds(row0, 8), :], va[pl.ds(row0 + 8, 8), :]
    ia_lo, ia_hi = ia[pl.ds(row0, 8), :], ia[pl.ds(row0 + 8, 8), :]
    b16, b8 = vb[pl.ds(row0, PEER_TOPK), :], vb[pl.ds(row0, 8), :]
    ib16, ib8 = ib[pl.ds(row0, PEER_TOPK), :], ib[pl.ds(row0, 8), :]
    cand = [a_lo[0:1, :] + b16]
    cid = [ia_lo[0:1, :] * N_SUBKEYS + ib16]
    for i in range(1, 8):
        cand.append(a_lo[i:i + 1, :] + b8)
        cid.append(ia_lo[i:i + 1, :] * N_SUBKEYS + ib8)
    cand.append(a_hi + b8[0:1, :])
    cid.append(ia_hi * N_SUBKEYS + ib8[0:1, :])
    cand = jnp.concatenate(cand, axis=0) + cmask
    cid = jnp.concatenate(cid, axis=0)
    best = []
    for kk in range(PEER_TOPK):
        m = jnp.max(cand, axis=0, keepdims=True)
        pos = jnp.min(jnp.where(cand == m, cpos, float(PEER_TOPK * PEER_TOPK)), axis=0, keepdims=True)
        sel = cpos == pos
        best.append(m)
        ids[pl.ds(row0 + kk, 1), :] = jnp.max(jnp.where(sel, cid, -1.0), axis=0, keepdims=True)
        cand = jnp.where(sel, -jnp.inf, cand)
    best = jnp.concatenate(best, axis=0)
    ex = jnp.exp(best - best[0:1, :])
    gts[pl.ds(row0, PEER_TOPK), :] = ex / jnp.sum(ex, axis=0, keepdims=True)


def _peer_kernel(sc0_ref, sc_ref, pos_ref, mask_ref, hn_ref, h_ref, tab_hbm, o_ref,
                 buf0, buf1, sem, va, ia, vb, ib, ids, gts, idx_tiles, g_tiles, idx_smem, sem_idx, *, last_step, ntile):
    s = pl.program_id(0)
    d = hn_ref.shape[1]
    npair = d // (2 * LANES)
    group = PEER_SLOTS // (2 * npair)
    bufs = (buf0, buf1)
    tt = sc_ref.shape[2]
    iota = lax.broadcasted_iota(I32, (N_SUBKEYS, tt), 0).astype(F32)

    def stage1(ref, h, half):
        vref, iref = (va, ia) if half == 0 else (vb, ib)
        _topk_stage1(ref[2 * h + half], vref, iref, pl.multiple_of(h * PEER_TOPK, PEER_TOPK), iota)

    def stage2(h):
        _topk_stage2(va, ia, vb, ib, pl.multiple_of(h * PEER_TOPK, PEER_TOPK), pos_ref[...], mask_ref[...], ids, gts)

    def finalize(par):
        idx_tiles[par] = ids[...].T.astype(I32)
        g_tiles[par] = gts[...].T

    def idx_copy(block, slot):
        par = lax.rem(block // STEPS_PER_TILE, 2)
        row = pl.multiple_of(lax.rem(block, STEPS_PER_TILE) * EXPERT_TOKENS, EXPERT_TOKENS)
        return pltpu.make_async_copy(idx_tiles.at[par, pl.ds(row, EXPERT_TOKENS), :], idx_smem.at[slot], sem_idx.at[slot])

    def issue(dst, t, piece):
        for kk in range(piece * group, (piece + 1) * group):
            r = t * PEER_SLOTS + kk
            pltpu.make_async_copy(tab_hbm.at[idx_smem[dst, t, kk]], bufs[dst].at[pl.ds(r * ROW_WORDS, ROW_WORDS), :],
                                  sem.at[dst]).start(priority=kk % 2)

    def step(dst, do_issue, do_compute):
        src = 1 - dst
        buf = bufs[src]
        if do_issue:
            idx_copy(s, dst).wait()
        if do_compute:
            pltpu.make_async_copy(buf, buf, sem.at[src]).wait()
            blk = s - 1
            par = lax.rem(blk // STEPS_PER_TILE, 2)
            row = pl.multiple_of(lax.rem(blk, STEPS_PER_TILE) * EXPERT_TOKENS, EXPERT_TOKENS)
            gt = g_tiles[par, pl.ds(row, EXPERT_TOKENS), :].T
        ys = []
        for t in range(EXPERT_TOKENS):
            base = t * PEER_SLOTS * ROW_WORDS
            acc = None
            for p in range(npair):
                if do_issue:
                    issue(dst, t, p)
                if do_compute:
                    w = buf[pl.ds(base + p, PEER_SLOTS, stride=ROW_WORDS), :]
                    c0, c1 = 2 * p, 2 * p + 1
                    term = (_lo_f32(w) * hn_ref[t:t + 1, c0 * LANES:(c0 + 1) * LANES]
                            + _hi_f32(w) * hn_ref[t:t + 1, c1 * LANES:(c1 + 1) * LANES])
                    acc = term if acc is None else acc + term
            if do_compute:
                a = jnp.sum(acc, axis=1, keepdims=True)
                hcoef = jax.nn.gelu(a) * gt[:, t:t + 1]
            yt = []
            for p in range(npair):
                if do_issue:
                    issue(dst, t, npair + p)
                if do_compute:
                    w = buf[pl.ds(base + npair + p, PEER_SLOTS, stride=ROW_WORDS), :]
                    yt.append(jnp.sum(hcoef * _lo_f32(w), axis=0, keepdims=True))
                    yt.append(jnp.sum(hcoef * _hi_f32(w), axis=0, keepdims=True))
            if do_compute:
                ys.append(jnp.concatenate(yt, axis=1))
        if do_compute:
            o_ref[...] = h_ref[...] + jnp.concatenate(ys, axis=0)

    even = lax.rem(s, 2) == 0
    steady = jnp.logical_and(s > 0, s < last_step)

    @pl.when(s == 0)
    def _():
        def head(h, carry):
            stage1(sc0_ref, h, 0)
            stage1(sc0_ref, h, 1)
            stage2(h)
            return carry
        lax.fori_loop(0, PEER_HEADS, head, 0)
        finalize(0)
        idx_copy(0, 0).start()
        step(0, True, False)

    @pl.when(jnp.logical_and(steady, even))
    def _():
        step(0, True, True)

    @pl.when(jnp.logical_and(steady, jnp.logical_not(even)))
    def _():
        step(1, True, True)

    @pl.when(s == last_step)
    def _():
        step(last_step % 2, False, True)

    u = lax.rem(s, STEPS_PER_TILE)
    nxt = s // STEPS_PER_TILE + 1
    active = nxt < ntile

    @pl.when(jnp.logical_and(active, u < PEER_HEADS))
    def _():
        stage1(sc_ref, u, 0)
        stage1(sc_ref, u, 1)

    @pl.when(jnp.logical_and(active, u >= PEER_HEADS))
    def _():
        stage2(u - PEER_HEADS)

    @pl.when(jnp.logical_and(active, u == STEPS_PER_TILE - 1))
    def _():
        finalize(lax.rem(nxt, 2))

    @pl.when(s + 1 < last_step)
    def _():
        idx_copy(s + 1, lax.rem(s + 1, 2)).start()


def _peer(scores_t, hn, hres, tab):
    n_tok, d = hn.shape
    nsk = scores_t.shape[0]
    assert tab.shape[1:] == (ROW_WORDS, LANES) and d == ROW_WORDS * LANES
    assert n_tok % TOPK_TILE == 0 and STEPS_PER_TILE == 2 * PEER_HEADS
    nstep = n_tok // EXPERT_TOKENS
    ntile = n_tok // TOPK_TILE
    cpos, cmask = _cand_constants(TOPK_TILE)
    prev = lambda s: (jnp.maximum(s - 1, 0), 0)
    tokp = pl.BlockSpec((EXPERT_TOKENS, d), prev)
    const = pl.BlockSpec((CAND_ROWS, TOPK_TILE), lambda s: (0, 0))
    tile_f32 = pltpu.VMEM((PEER_SLOTS, TOPK_TILE), F32)
    buf = pltpu.VMEM((EXPERT_TOKENS * PEER_SLOTS * ROW_WORDS, LANES), I32)
    return pl.pallas_call(
        functools.partial(_peer_kernel, last_step=nstep, ntile=ntile), grid=(nstep + 1,),
        out_shape=jax.ShapeDtypeStruct((n_tok, d), F32),
        in_specs=[pl.BlockSpec((nsk, N_SUBKEYS, TOPK_TILE), lambda s: (0, 0, 0)),
                  pl.BlockSpec((nsk, N_SUBKEYS, TOPK_TILE),
                               lambda s: (0, 0, jnp.minimum(s // STEPS_PER_TILE + 1, ntile - 1))),
                  const, const, tokp, tokp, pl.BlockSpec(memory_space=pl.ANY)],
        out_specs=tokp,
        scratch_shapes=[buf, buf, pltpu.SemaphoreType.DMA((2,)),
                        tile_f32, tile_f32, tile_f32, tile_f32, tile_f32, tile_f32,
                        pltpu.VMEM((2, TOPK_TILE, PEER_SLOTS), I32), pltpu.VMEM((2, TOPK_TILE, PEER_SLOTS), F32),
                        pltpu.SMEM((2, EXPERT_TOKENS, PEER_SLOTS), I32), pltpu.SemaphoreType.DMA((2,))],
        name="peer",
        compiler_params=pltpu.CompilerParams(dimension_semantics=("arbitrary",), vmem_limit_bytes=VMEM_LIMIT),
    )(scores_t, scores_t, cpos, cmask, hn, hres, tab)


def _layer(x, seqs, norm1_g, w_in, b_igate, b_fgate, mlstm_norm_g, qk_norm_q, qk_norm_k, rpb,
           w_branch_a, w_branch_b, w_out, norm2_g, peer_wq, peer_subkeys, peer_u, peer_v):
    n_tok, d = x.shape
    tm = 256
    row = lambda a: a.astype(F32).reshape(1, -1)

    c0, c1, c2, c3 = 3 * M_WIDTH, 4 * M_WIDTH, 4 * M_WIDTH + 4 * M_HEADS, 4 * M_WIDTH + 4 * M_HEADS + 3 * N_WIDTH
    wm = w_in[:, 0:c1].astype(BF16)
    wg = jnp.pad(w_in[:, c1:c2], ((0, 0), (0, LANES - 4 * M_HEADS))).astype(BF16)
    wn = w_in[:, c2:c3].astype(BF16)
    wmg = w_in[:, c3:].astype(BF16)
    head_of = np.arange(N_WIDTH) // N_HEAD_DIM
    bd = jnp.asarray((head_of[:, None] == head_of[None, :]).astype(np.float32) / N_HEAD_DIM, BF16)
    gq = jnp.tile(qk_norm_q.astype(F32), N_HEADS).reshape(1, N_WIDTH)
    gk = jnp.tile(qk_norm_k.astype(F32), N_HEADS).reshape(1, N_WIDTH)
    gate_bias = jnp.pad(jnp.concatenate([b_igate.reshape(-1), b_fgate.reshape(-1)]).astype(F32),
                        (0, LANES - 4 * M_HEADS)).reshape(1, LANES)
    bias_tab = _natten_bias_table(rpb)
    half = PEER_KEY_DIM // 2
    sk = peer_subkeys.astype(F32).reshape(2 * PEER_HEADS, N_SUBKEYS, half)
    z = jnp.zeros_like(sk)
    odd = (np.arange(2 * PEER_HEADS) % 2 == 1)[:, None, None]
    skp = jnp.where(odd, jnp.concatenate([z, sk], axis=-1), jnp.concatenate([sk, z], axis=-1)).astype(BF16)
    uv = jnp.concatenate([_pack_rows(peer_u), _pack_rows(peer_v)], axis=1)

    seq_chunks = tuple((s // M_CHUNK, (s + t) // M_CHUNK) for s, t in seqs)
    seq_blocks = tuple((s // NA_BLOCK, (s + t) // NA_BLOCK) for s, t in seqs)

    mq, mk, mv, mo, gates, nq, nk, nv, merge = _proj(x, row(norm1_g), wm, wg, wn, wmg, bd, gq, gk, tm)
    hf, hb = _mlstm(mq, mk, mv, gates, gate_bias, seq_chunks)
    na = _natten(nq, nk, nv, bias_tab, seq_blocks)
    hres, hn, scores_t = _mix(x, hf, hb, mo, na, merge, row(mlstm_norm_g), w_branch_a.astype(BF16),
                              w_branch_b.astype(BF16), w_out.astype(BF16), row(norm2_g), peer_wq.astype(BF16), skp, tm)
    return _peer(scores_t, hn, hres, uv)


def kernel(x_prompt, x_sample, norm1_g, w_in, b_igate, b_fgate, mlstm_norm_g, qk_norm_q, qk_norm_k, rpb,
           w_branch_a, w_branch_b, w_out, norm2_g, peer_wq, peer_subkeys, peer_u, peer_v):
    depth = w_in.shape[0]
    d = x_prompt.shape[-1]
    groups = (x_prompt, x_sample)
    seqs, start = [], 0
    for gx in groups:
        for _ in range(gx.shape[0]):
            seqs.append((start, gx.shape[1]))
            start += gx.shape[1]
    for s, t in seqs:
        assert t % NA_BLOCK == 0 and t % M_CHUNK == 0 and t // GRID_W >= WIN_H
    x = jnp.concatenate([gx.reshape(-1, d) for gx in groups], axis=0)
    for l in range(depth):
        x = _layer(x, tuple(seqs), norm1_g[l], w_in[l], b_igate[l], b_fgate[l], mlstm_norm_g[l], qk_norm_q[l],
                   qk_norm_k[l], rpb[l], w_branch_a[l], w_branch_b[l], w_out[l], norm2_g[l], peer_wq[l],
                   peer_subkeys[l], peer_u[l], peer_v[l])
    outs, start = [], 0
    for gx in groups:
        cnt = gx.shape[0] * gx.shape[1]
        outs.append(x[start:start + cnt].reshape(gx.shape))
        start += cnt
    return tuple(outs)
```

```python
import functools

import numpy as np
import jax
import jax.numpy as jnp
from jax import lax
from jax.experimental import pallas as pl
from jax.experimental.pallas import tpu as pltpu

F32 = jnp.float32
BF16 = jnp.bfloat16
I32 = jnp.int32

EPS = 1e-6
GRID_W = 64
M_HEADS = 4
M_HEAD_DIM = 128
M_WIDTH = M_HEADS * M_HEAD_DIM
M_CHUNK = 128
N_HEADS = 8
N_HEAD_DIM = 64
N_WIDTH = N_HEADS * N_HEAD_DIM
WIN_H = 8
WIN_W = 16
PEER_HEADS = 8
N_SUBKEYS = 128
PEER_KEY_DIM = 128
PEER_TOPK = 16
PEER_SLOTS = PEER_HEADS * PEER_TOPK

LANES = 128
VMEM_LIMIT = 48 * 1024 * 1024

NA_ROWS_PER_STEP = 8
NA_BLOCK = NA_ROWS_PER_STEP * GRID_W
NA_CLASSES = WIN_H
NEG_BIG = -1e30


def _nt_dot(a, b):
    return lax.dot_general(a, b, (((1,), (1,)), ((), ())), preferred_element_type=F32)


def _dot(a, b):
    return jnp.dot(a, b, preferred_element_type=F32)


def _split_bf16(a):
    hi = a.astype(BF16)
    lo = (a - hi.astype(F32)).astype(BF16)
    return hi, lo


def _pair_specs(tile, width, first_tiles, shift=0):
    first = pl.BlockSpec((tile, width), lambda i: (jnp.clip(i - shift, 0, first_tiles - 1), 0))
    second = pl.BlockSpec((tile, width), lambda i: (jnp.maximum(i - shift - first_tiles, 0), 0))
    return first, second


def _proj_kernel(xa_ref, xb_ref, g1_ref, wm_ref, wg_ref, wn_ref, wmg_ref, bd_ref, gq_ref, gk_ref,
                 mq_ref, mk_ref, mv_ref, mo_ref, gates_ref, nq_ref, nk_ref, nv_ref, merge_ref, *, first_tiles):
    x = jnp.where(pl.program_id(0) < first_tiles, xa_ref[...], xb_ref[...])
    ms = jnp.mean(x * x, axis=-1, keepdims=True)
    xn = (x * lax.rsqrt(ms + EPS) * g1_ref[...]).astype(BF16)

    m = _dot(xn, wm_ref[...])
    mq_ref[...] = (m[:, 0:M_WIDTH] * (M_HEAD_DIM ** -0.5)).astype(BF16)
    mk_ref[...] = m[:, M_WIDTH:2 * M_WIDTH].astype(BF16)
    mv_ref[...] = m[:, 2 * M_WIDTH:3 * M_WIDTH].astype(BF16)
    mo_ref[...] = m[:, 3 * M_WIDTH:4 * M_WIDTH]
    gates_ref[...] = _dot(xn, wg_ref[...])

    n = _dot(xn, wn_ref[...])
    bd = bd_ref[...]

    def head_norm(t, g):
        hi, lo = _split_bf16(t * t)
        msq = _dot(hi, bd) + _dot(lo, bd)
        return t * lax.rsqrt(msq + EPS) * g

    nq_ref[...] = (head_norm(n[:, 0:N_WIDTH], gq_ref[...]) * (N_HEAD_DIM ** -0.5)).astype(BF16)
    nk_ref[...] = head_norm(n[:, N_WIDTH:2 * N_WIDTH], gk_ref[...]).astype(BF16)
    nv_ref[...] = n[:, 2 * N_WIDTH:3 * N_WIDTH].astype(BF16)
    merge_ref[...] = _dot(xn, wmg_ref[...])


def _proj(xa, xb, g1, wm, wg, wn, wmg, bd, gq, gk, tm):
    d = xa.shape[1]
    n_tok = xa.shape[0] + xb.shape[0]
    first_tiles = xa.shape[0] // tm
    grid = (n_tok // tm,)
    tok = lambda w: pl.BlockSpec((tm, w), lambda i: (i, 0))
    full = lambda a: pl.BlockSpec(a.shape, lambda i: (0,) * a.ndim)
    out_shape = (
        jax.ShapeDtypeStruct((n_tok, M_WIDTH), BF16),
        jax.ShapeDtypeStruct((n_tok, M_WIDTH), BF16),
        jax.ShapeDtypeStruct((n_tok, M_WIDTH), BF16),
        jax.ShapeDtypeStruct((n_tok, M_WIDTH), F32),
        jax.ShapeDtypeStruct((n_tok, LANES), F32),
        jax.ShapeDtypeStruct((n_tok, N_WIDTH), BF16),
        jax.ShapeDtypeStruct((n_tok, N_WIDTH), BF16),
        jax.ShapeDtypeStruct((n_tok, N_WIDTH), BF16),
        jax.ShapeDtypeStruct((n_tok, 2 * d), F32),
    )
    out_specs = (tok(M_WIDTH), tok(M_WIDTH), tok(M_WIDTH), tok(M_WIDTH), tok(LANES),
                 tok(N_WIDTH), tok(N_WIDTH), tok(N_WIDTH), tok(2 * d))
    return pl.pallas_call(
        functools.partial(_proj_kernel, first_tiles=first_tiles), grid=grid, out_shape=out_shape,
        in_specs=[*_pair_specs(tm, d, first_tiles),
                  full(g1), full(wm), full(wg), full(wn), full(wmg), full(bd), full(gq), full(gk)],
        out_specs=out_specs, name="proj",
        compiler_params=pltpu.CompilerParams(dimension_semantics=("arbitrary",), vmem_limit_bytes=VMEM_LIMIT),
    )(xa, xb, g1, wm, wg, wn, wmg, bd, gq, gk)


def _log_sigmoid(z):
    return jnp.minimum(z, 0.0) - jnp.log1p(jnp.exp(-jnp.abs(z)))


def _mlstm_direction(q_ref, k_ref, v_ref, g_ref, bias_ref, out_ref, c_ref, n_ref, m_ref, direction):
    L = M_CHUNK
    g = g_ref[...] + bias_ref[...]
    lane = lax.broadcasted_iota(I32, (L, LANES), 1)
    lg = jnp.where(lane >= 2 * M_HEADS, _log_sigmoid(g), g)
    t_idx = lax.broadcasted_iota(I32, (L, L), 0)
    s_idx = lax.broadcasted_iota(I32, (L, L), 1)
    causal = (s_idx <= t_idx) if direction == 0 else (s_idx >= t_idx)
    tri = causal.astype(BF16)
    hi, lo = _split_bf16(lg)
    bc = _dot(tri, hi) + _dot(tri, lo)
    bt = bc.T
    lgt = lg.T
    for h in range(M_HEADS):
        u = direction * M_HEADS + h
        cf = 2 * M_HEADS + direction * M_HEADS + h
        ci = direction * M_HEADS + h
        bcol = bc[:, cf:cf + 1]
        brow = bt[cf:cf + 1, :]
        li_row = lgt[ci:ci + 1, :]
        li_col = lg[:, ci:ci + 1]
        btot = brow[:, L - 1:L] if direction == 0 else brow[:, 0:1]
        hs = slice(h * M_HEAD_DIM, (h + 1) * M_HEAD_DIM)
        q = q_ref[:, hs]
        k = k_ref[:, hs]
        v = v_ref[:, hs]
        m_prev = m_ref[u:u + 1, 0:1]
        c_prev = c_ref[u]
        n_prev = n_ref[u:u + 1, :]

        d = jnp.where(causal, bcol - brow + li_row, -jnp.inf)
        a_inter = bcol + m_prev
        m_t = jnp.maximum(a_inter, jnp.max(d, axis=1, keepdims=True))
        p = _nt_dot(q, k) * jnp.exp(d - m_t)
        w_inter = jnp.exp(a_inter - m_t)
        num = _dot(p.astype(BF16), v) + w_inter * _nt_dot(q, c_prev.astype(BF16))
        den = jnp.sum(p, axis=1, keepdims=True) + w_inter * jnp.sum(q.astype(F32) * n_prev, axis=1, keepdims=True)
        out_ref[:, hs] = num / jnp.maximum(jnp.abs(den), jnp.exp(-m_t))

        w_end = btot - bcol + li_col
        m_loc = jnp.max(w_end, axis=0, keepdims=True)
        m_new = jnp.maximum(btot + m_prev, m_loc)
        decay = jnp.exp(btot + m_prev - m_new)
        e = jnp.exp(w_end - m_new)
        ev = (e * v.astype(F32)).T.astype(BF16)
        c_ref[u] = decay * c_prev + _dot(ev, k)
        n_ref[u:u + 1, :] = decay * n_prev + jnp.sum(e * k.astype(F32), axis=0, keepdims=True)
        m_ref[u:u + 1, :] = jnp.broadcast_to(m_new, (1, LANES))


def _mlstm_kernel(qf, kf, vf, gf, qb, kb, vb, gb, bias_ref, hf_ref, hb_ref, c_ref, n_ref, m_ref, *, seq_chunks):
    j = pl.program_id(0)
    jb = pl.num_programs(0) - 1 - j
    start_f = functools.reduce(jnp.logical_or, [j == s for s, _ in seq_chunks])
    start_b = functools.reduce(jnp.logical_or, [jb == e - 1 for _, e in seq_chunks])

    def reset(lo):
        c_ref[lo:lo + M_HEADS] = jnp.zeros((M_HEADS, M_HEAD_DIM, M_HEAD_DIM), F32)
        n_ref[lo:lo + M_HEADS, :] = jnp.zeros((M_HEADS, LANES), F32)
        m_ref[lo:lo + M_HEADS, :] = jnp.zeros((M_HEADS, LANES), F32)

    @pl.when(start_f)
    def _():
        reset(0)

    @pl.when(start_b)
    def _():
        reset(M_HEADS)

    _mlstm_direction(qf, kf, vf, gf, bias_ref, hf_ref, c_ref, n_ref, m_ref, 0)
    _mlstm_direction(qb, kb, vb, gb, bias_ref, hb_ref, c_ref, n_ref, m_ref, 1)


def _mlstm(mq, mk, mv, gates, gate_bias, seq_chunks):
    n_tok = mq.shape[0]
    nchunk = n_tok // M_CHUNK
    fwd = lambda w: pl.BlockSpec((M_CHUNK, w), lambda j: (j, 0))
    bwd = lambda w: pl.BlockSpec((M_CHUNK, w), lambda j: (nchunk - 1 - j, 0))
    kern = functools.partial(_mlstm_kernel, seq_chunks=seq_chunks)
    return pl.pallas_call(
        kern, grid=(nchunk,),
        out_shape=(jax.ShapeDtypeStruct((n_tok, M_WIDTH), F32), jax.ShapeDtypeStruct((n_tok, M_WIDTH), F32)),
        in_specs=[fwd(M_WIDTH), fwd(M_WIDTH), fwd(M_WIDTH), fwd(LANES),
                  bwd(M_WIDTH), bwd(M_WIDTH), bwd(M_WIDTH), bwd(LANES),
                  pl.BlockSpec((1, LANES), lambda j: (0, 0))],
        out_specs=(fwd(M_WIDTH), bwd(M_WIDTH)),
        scratch_shapes=[pltpu.VMEM((2 * M_HEADS, M_HEAD_DIM, M_HEAD_DIM), F32),
                        pltpu.VMEM((2 * M_HEADS, LANES), F32),
                        pltpu.VMEM((2 * M_HEADS, LANES), F32)],
        name="mlstm",
        compiler_params=pltpu.CompilerParams(dimension_semantics=("arbitrary",), vmem_limit_bytes=VMEM_LIMIT),
    )(mq, mk, mv, gates, mq, mk, mv, gates, gate_bias)


def _seq_block_scalars(j, seq_blocks):
    start = jnp.int32(0)
    count = jnp.int32(seq_blocks[0][1] - seq_blocks[0][0])
    for s, e in seq_blocks[1:]:
        inside = j >= s
        start = jnp.where(inside, s, start)
        count = jnp.where(inside, e - s, count)
    return j - start, count


def _natten_kernel(q_ref, kp_ref, kc_ref, kn_ref, vp_ref, vc_ref, vn_ref, bias_ref, o_ref, kcat, vcat, *, seq_blocks):
    j = pl.program_id(1)
    jl, nb = _seq_block_scalars(j, seq_blocks)
    rows = nb * NA_ROWS_PER_STEP
    kcat[0:NA_BLOCK, :] = kp_ref[...]
    kcat[NA_BLOCK:2 * NA_BLOCK, :] = kc_ref[...]
    kcat[2 * NA_BLOCK:3 * NA_BLOCK, :] = kn_ref[...]
    vcat[0:NA_BLOCK, :] = vp_ref[...]
    vcat[NA_BLOCK:2 * NA_BLOCK, :] = vc_ref[...]
    vcat[2 * NA_BLOCK:3 * NA_BLOCK, :] = vn_ref[...]
    lane = lax.broadcasted_iota(I32, (GRID_W, LANES), 1)
    first_head = lane < N_HEAD_DIM

    def row_body(i, carry):
        r = jl * NA_ROWS_PER_STEP + i
        r0 = jnp.clip(r - WIN_H // 2, 0, rows - WIN_H)
        cls = r0 - r + (WIN_H - 1)
        ls = pl.multiple_of((r0 - jl * NA_ROWS_PER_STEP + NA_ROWS_PER_STEP) * GRID_W, GRID_W)
        qi = q_ref[pl.ds(pl.multiple_of(i * GRID_W, GRID_W), GRID_W), :]
        kband = kcat[pl.ds(ls, WIN_H * GRID_W), :]
        vband = vcat[pl.ds(ls, WIN_H * GRID_W), :]
        zero = jnp.zeros_like(qi)
        q2 = jnp.concatenate([jnp.where(first_head, qi, zero), jnp.where(first_head, zero, qi)], axis=0)
        s = _nt_dot(q2, kband) + bias_ref[cls]
        mx = jnp.max(s, axis=1, keepdims=True)
        p = jnp.exp(s - mx)
        den = jnp.sum(p, axis=1, keepdims=True)
        o2 = _dot(p.astype(BF16), vband) / den
        o_ref[pl.ds(pl.multiple_of(i * GRID_W, GRID_W), GRID_W), :] = jnp.where(
            first_head, o2[0:GRID_W], o2[GRID_W:2 * GRID_W]).astype(o_ref.dtype)
        return carry

    lax.fori_loop(0, NA_ROWS_PER_STEP, row_body, 0, unroll=True)


def _natten(nq, nk, nv, bias_tab, seq_blocks):
    n_tok = nq.shape[0]
    nblk = n_tok // NA_BLOCK
    npair = N_HEADS // 2

    def nbr_map(delta):
        def index_map(p, j):
            lo = jnp.int32(seq_blocks[0][0])
            hi = jnp.int32(seq_blocks[0][1] - 1)
            for s, e in seq_blocks[1:]:
                inside = j >= s
                lo = jnp.where(inside, s, lo)
                hi = jnp.where(inside, e - 1, hi)
            return (jnp.clip(j + delta, lo, hi), p)
        return index_map

    blk = lambda im: pl.BlockSpec((NA_BLOCK, LANES), im)
    cur = lambda p, j: (j, p)
    kern = functools.partial(_natten_kernel, seq_blocks=seq_blocks)
    return pl.pallas_call(
        kern, grid=(npair, nblk),
        out_shape=jax.ShapeDtypeStruct((n_tok, N_WIDTH), BF16),
        in_specs=[blk(cur), blk(nbr_map(-1)), blk(cur), blk(nbr_map(1)),
                  blk(nbr_map(-1)), blk(cur), blk(nbr_map(1)),
                  pl.BlockSpec((None, NA_CLASSES, 2 * GRID_W, WIN_H * GRID_W), lambda p, j: (p, 0, 0, 0))],
        out_specs=blk(cur),
        scratch_shapes=[pltpu.VMEM((3 * NA_BLOCK, LANES), BF16), pltpu.VMEM((3 * NA_BLOCK, LANES), BF16)],
        name="natten",
        compiler_params=pltpu.CompilerParams(dimension_semantics=("arbitrary", "arbitrary"), vmem_limit_bytes=VMEM_LIMIT),
    )(nq, nk, nk, nk, nv, nv, nv, bias_tab)


def _natten_bias_table(rpb):
    c = np.arange(GRID_W)
    c0 = np.clip(c - WIN_W // 2, 0, GRID_W - WIN_W)
    kc = np.arange(GRID_W)
    valid = (kc[None, :] >= c0[:, None]) & (kc[None, :] < c0[:, None] + WIN_W)
    dc = np.clip(kc[None, :] - c[:, None] + WIN_W - 1, 0, 2 * WIN_W - 2)
    onehot = (dc.reshape(-1)[None, :] == np.arange(2 * WIN_W - 1)[:, None]).astype(np.float32)
    e1 = jnp.einsum('hrd,dx->hrx', rpb.astype(F32), jnp.asarray(onehot), precision=lax.Precision.HIGHEST)
    e1 = e1.reshape(N_HEADS, 2 * WIN_H - 1, GRID_W, GRID_W)
    e1 = jnp.where(valid[None, None], e1, NEG_BIG)
    tab = jnp.stack([e1[:, cls:cls + WIN_H] for cls in range(NA_CLASSES)], axis=1)
    tab = tab.transpose(0, 1, 3, 2, 4)
    tab = tab.reshape(N_HEADS // 2, 2, NA_CLASSES, GRID_W, WIN_H * GRID_W).transpose(0, 2, 1, 3, 4)
    return tab.reshape(N_HEADS // 2, NA_CLASSES, 2 * GRID_W, WIN_H * GRID_W)


def _mix_kernel(xa_ref, xb_ref, hf_ref, hb_ref, mo_ref, na_ref, mg_ref, gh_ref, wa_ref, wb_ref, wo_ref, g2_ref, wq_ref,
                sk_ref, h_ref, hn_ref, st_ref, *, first_tiles):
    d = xa_ref.shape[1]
    x = jnp.where(pl.program_id(0) < first_tiles, xa_ref[...], xb_ref[...])
    hm = hf_ref[...] + hb_ref[...]
    parts = []
    for h in range(M_HEADS):
        hs = hm[:, h * M_HEAD_DIM:(h + 1) * M_HEAD_DIM]
        parts.append(hs * lax.rsqrt(jnp.mean(hs * hs, axis=-1, keepdims=True) + EPS))
    hm = jnp.concatenate(parts, axis=1) * gh_ref[...]
    ya = _dot((jax.nn.sigmoid(mo_ref[...]) * hm).astype(BF16), wa_ref[...])
    yb = _dot(na_ref[...], wb_ref[...])
    mg = mg_ref[...]
    mixed = jax.nn.sigmoid(mg[:, 0:d]) * ya + jax.nn.sigmoid(mg[:, d:2 * d]) * yb
    hres = x + _dot(mixed.astype(BF16), wo_ref[...])
    h_ref[...] = hres
    hn = hres * lax.rsqrt(jnp.mean(hres * hres, axis=-1, keepdims=True) + EPS) * g2_ref[...]
    hn_ref[...] = hn
    qp = _dot(hn.astype(BF16), wq_ref[...]).astype(BF16)
    for b in range(2 * PEER_HEADS):
        lb = b // 2
        st_ref[b] = _nt_dot(sk_ref[b], qp[:, lb * LANES:(lb + 1) * LANES])


def _mix(xa, xb, hf, hb, mo, na, merge, gh, wa, wb, wo, g2, wq, skp, tm):
    d = xa.shape[1]
    n_tok = xa.shape[0] + xb.shape[0]
    first_tiles = xa.shape[0] // tm
    tok = lambda w: pl.BlockSpec((tm, w), lambda i: (i, 0))
    full = lambda a: pl.BlockSpec(a.shape, lambda i: (0,) * a.ndim)
    nsk = 2 * PEER_HEADS
    return pl.pallas_call(
        functools.partial(_mix_kernel, first_tiles=first_tiles), grid=(n_tok // tm,),
        out_shape=(jax.ShapeDtypeStruct((n_tok, d), F32), jax.ShapeDtypeStruct((n_tok, d), F32),
                   jax.ShapeDtypeStruct((nsk, N_SUBKEYS, n_tok), F32)),
        in_specs=[*_pair_specs(tm, d, first_tiles),
                  tok(M_WIDTH), tok(M_WIDTH), tok(M_WIDTH), tok(N_WIDTH), tok(2 * d),
                  full(gh), full(wa), full(wb), full(wo), full(g2), full(wq), full(skp)],
        out_specs=(tok(d), tok(d), pl.BlockSpec((nsk, N_SUBKEYS, tm), lambda i: (0, 0, i))),
        name="mix",
        compiler_params=pltpu.CompilerParams(dimension_semantics=("arbitrary",), vmem_limit_bytes=VMEM_LIMIT),
    )(xa, xb, hf, hb, mo, na, merge, gh, wa, wb, wo, g2, wq, skp)


CAND_ROWS = PEER_TOPK + 7 * 8 + 8


def _cand_constants(tt):
    ii = np.concatenate([np.zeros(16), np.repeat(np.arange(1, 8), 8), np.arange(8, 16)]).astype(np.int64)
    jj = np.concatenate([np.arange(16), np.tile(np.arange(8), 7), np.zeros(8)]).astype(np.int64)
    valid = (ii + 1) * (jj + 1) <= PEER_TOPK
    pos = (ii * PEER_TOPK + jj).astype(np.float32)
    mask = np.where(valid, 0.0, -np.inf).astype(np.float32)
    return (jnp.asarray(np.broadcast_to(pos[:, None], (CAND_ROWS, tt))),
            jnp.asarray(np.broadcast_to(mask[:, None], (CAND_ROWS, tt))))


EXPERT_TOKENS = 16
ROW_WORDS = 8


def _lo_f32(w):
    return lax.bitcast_convert_type(lax.shift_left(w, 16), F32)


def _hi_f32(w):
    return lax.bitcast_convert_type(jnp.bitwise_and(w, jnp.int32(-65536)), F32)


PACK_ROWS = 256


def _pack_kernel(u_ref, v_ref, o_ref):
    npair = u_ref.shape[1] // (2 * LANES)

    def words(ref, p):
        lo = ref[:, (2 * p) * LANES:(2 * p + 1) * LANES].astype(BF16).astype(F32)
        hi = ref[:, (2 * p + 1) * LANES:(2 * p + 2) * LANES].astype(BF16).astype(F32)
        lo_bits = lax.shift_right_logical(lax.bitcast_convert_type(lo, I32), 16)
        return jnp.bitwise_or(lo_bits, jnp.bitwise_and(lax.bitcast_convert_type(hi, I32), jnp.int32(-65536)))

    for p in range(npair):
        o_ref[:, p, :] = words(u_ref, p)
        o_ref[:, npair + p, :] = words(v_ref, p)


def _pack_table(u, v):
    e, d = u.shape
    assert d == ROW_WORDS * LANES and e % PACK_ROWS == 0
    src = pl.BlockSpec((PACK_ROWS, d), lambda i: (i, 0))
    return pl.pallas_call(
        _pack_kernel, grid=(e // PACK_ROWS,),
        out_shape=jax.ShapeDtypeStruct((e, ROW_WORDS, LANES), I32),
        in_specs=[src, src], out_specs=pl.BlockSpec((PACK_ROWS, ROW_WORDS, LANES), lambda i: (i, 0, 0)),
        name="pack",
        compiler_params=pltpu.CompilerParams(dimension_semantics=("arbitrary",), vmem_limit_bytes=VMEM_LIMIT),
    )(u, v)


TOPK_TILE = 2 * LANES
STEPS_PER_TILE = TOPK_TILE // EXPERT_TOKENS


def _topk_stage1(s, vref, iref, row0, iota):
    for kk in range(PEER_TOPK):
        m = jnp.max(s, axis=0, keepdims=True)
        pos = jnp.min(jnp.where(s == m, iota, float(N_SUBKEYS)), axis=0, keepdims=True)
        vref[pl.ds(row0 + kk, 1), :] = m
        iref[pl.ds(row0 + kk, 1), :] = pos
        s = jnp.where(iota == pos, -jnp.inf, s)


def _topk_stage2(va, ia, vb, ib, row0, cpos, cmask, ids, gts):
    a_lo, a_hi = va[pl.ds(row0, 8), :], va[pl.ds(row0 + 8, 8), :]
    ia_lo, ia_hi = ia[pl.ds(row0, 8), :], ia[pl.ds(row0 + 8, 8), :]
    b16, b8 = vb[pl.ds(row0, PEER_TOPK), :], vb[pl.ds(row0, 8), :]
    ib16, ib8 = ib[pl.ds(row0, PEER_TOPK), :], ib[pl.ds(row0, 8), :]
    cand = [a_lo[0:1, :] + b16]
    cid = [ia_lo[0:1, :] * N_SUBKEYS + ib16]
    for i in range(1, 8):
        cand.append(a_lo[i:i + 1, :] + b8)
        cid.append(ia_lo[i:i + 1, :] * N_SUBKEYS + ib8)
    cand.append(a_hi + b8[0:1, :])
    cid.append(ia_hi * N_SUBKEYS + ib8[0:1, :])
    cand = jnp.concatenate(cand, axis=0) + cmask
    cid = jnp.concatenate(cid, axis=0)
    best = []
    for kk in range(PEER_TOPK):
        m = jnp.max(cand, axis=0, keepdims=True)
        pos = jnp.min(jnp.where(cand == m, cpos, float(PEER_TOPK * PEER_TOPK)), axis=0, keepdims=True)
        sel = cpos == pos
        best.append(m)
        ids[pl.ds(row0 + kk, 1), :] = jnp.max(jnp.where(sel, cid, -1.0), axis=0, keepdims=True)
        cand = jnp.where(sel, -jnp.inf, cand)
    best = jnp.concatenate(best, axis=0)
    ex = jnp.exp(best - best[0:1, :])
    gts[pl.ds(row0, PEER_TOPK), :] = ex / jnp.sum(ex, axis=0, keepdims=True)


def _peer_kernel(sc0_ref, sc_ref, pos_ref, mask_ref, hn_ref, h_ref, tab_hbm, oa_ref, ob_ref,
                 buf0, buf1, sem, va, ia, vb, ib, ids, gts, idx_tiles, g_tiles, idx_smem, sem_idx,
                 *, last_step, ntile, first_blocks):
    s = pl.program_id(0)
    d = hn_ref.shape[1]
    npair = d // (2 * LANES)
    group = PEER_SLOTS // (2 * npair)
    bufs = (buf0, buf1)
    tt = sc_ref.shape[2]
    iota = lax.broadcasted_iota(I32, (N_SUBKEYS, tt), 0).astype(F32)

    def stage1(ref, h, half):
        vref, iref = (va, ia) if half == 0 else (vb, ib)
        _topk_stage1(ref[2 * h + half], vref, iref, pl.multiple_of(h * PEER_TOPK, PEER_TOPK), iota)

    def stage2(h):
        _topk_stage2(va, ia, vb, ib, pl.multiple_of(h * PEER_TOPK, PEER_TOPK), pos_ref[...], mask_ref[...], ids, gts)

    def finalize(par):
        idx_tiles[par] = ids[...].T.astype(I32)
        g_tiles[par] = gts[...].T

    def idx_copy(block, slot):
        par = lax.rem(block // STEPS_PER_TILE, 2)
        row = pl.multiple_of(lax.rem(block, STEPS_PER_TILE) * EXPERT_TOKENS, EXPERT_TOKENS)
        return pltpu.make_async_copy(idx_tiles.at[par, pl.ds(row, EXPERT_TOKENS), :], idx_smem.at[slot], sem_idx.at[slot])

    def issue(dst, t, piece):
        for kk in range(piece * group, (piece + 1) * group):
            r = t * PEER_SLOTS + kk
            pltpu.make_async_copy(tab_hbm.at[idx_smem[dst, t, kk]], bufs[dst].at[pl.ds(r * ROW_WORDS, ROW_WORDS), :],
                                  sem.at[dst]).start(priority=kk % 2)

    def step(dst, do_issue, do_compute):
        src = 1 - dst
        buf = bufs[src]
        if do_issue:
            idx_copy(s, dst).wait()
        if do_compute:
            pltpu.make_async_copy(buf, buf, sem.at[src]).wait()
            blk = s - 1
            par = lax.rem(blk // STEPS_PER_TILE, 2)
            row = pl.multiple_of(lax.rem(blk, STEPS_PER_TILE) * EXPERT_TOKENS, EXPERT_TOKENS)
            gt = g_tiles[par, pl.ds(row, EXPERT_TOKENS), :].T
        ys = []
        for t in range(EXPERT_TOKENS):
            base = t * PEER_SLOTS * ROW_WORDS
            acc = None
            for p in range(npair):
                if do_issue:
                    issue(dst, t, p)
                if do_compute:
                    w = buf[pl.ds(base + p, PEER_SLOTS, stride=ROW_WORDS), :]
                    c0, c1 = 2 * p, 2 * p + 1
                    term = (_lo_f32(w) * hn_ref[t:t + 1, c0 * LANES:(c0 + 1) * LANES]
                            + _hi_f32(w) * hn_ref[t:t + 1, c1 * LANES:(c1 + 1) * LANES])
                    acc = term if acc is None else acc + term
            if do_compute:
                a = jnp.sum(acc, axis=1, keepdims=True)
                hcoef = jax.nn.gelu(a) * gt[:, t:t + 1]
            yt = []
            for p in range(npair):
                if do_issue:
                    issue(dst, t, npair + p)
                if do_compute:
                    w = buf[pl.ds(base + npair + p, PEER_SLOTS, stride=ROW_WORDS), :]
                    yt.append(jnp.sum(hcoef * _lo_f32(w), axis=0, keepdims=True))
                    yt.append(jnp.sum(hcoef * _hi_f32(w), axis=0, keepdims=True))
            if do_compute:
                ys.append(jnp.concatenate(yt, axis=1))
        if do_compute:
            y = h_ref[...] + jnp.concatenate(ys, axis=0)
            in_first = s - 1 < first_blocks

            @pl.when(in_first)
            def _():
                oa_ref[...] = y

            @pl.when(jnp.logical_not(in_first))
            def _():
                ob_ref[...] = y

    even = lax.rem(s, 2) == 0
    steady = jnp.logical_and(s > 0, s < last_step)

    @pl.when(s == 0)
    def _():
        def head(h, carry):
            stage1(sc0_ref, h, 0)
            stage1(sc0_ref, h, 1)
            stage2(h)
            return carry
        lax.fori_loop(0, PEER_HEADS, head, 0)
        finalize(0)
        idx_copy(0, 0).start()
        step(0, True, False)

    @pl.when(jnp.logical_and(steady, even))
    def _():
        step(0, True, True)

    @pl.when(jnp.logical_and(steady, jnp.logical_not(even)))
    def _():
        step(1, True, True)

    @pl.when(s == last_step)
    def _():
        step(last_step % 2, False, True)

    u = lax.rem(s, STEPS_PER_TILE)
    nxt = s // STEPS_PER_TILE + 1
    active = nxt < ntile

    if STEPS_PER_TILE == PEER_HEADS:
        @pl.when(active)
        def _():
            stage1(sc_ref, u, 0)
            stage1(sc_ref, u, 1)
            stage2(u)
    else:
        @pl.when(jnp.logical_and(active, u < PEER_HEADS))
        def _():
            stage1(sc_ref, u, 0)
            stage1(sc_ref, u, 1)

        @pl.when(jnp.logical_and(active, u >= PEER_HEADS))
        def _():
            stage2(u - PEER_HEADS)

    @pl.when(jnp.logical_and(active, u == STEPS_PER_TILE - 1))
    def _():
        finalize(lax.rem(nxt, 2))

    @pl.when(s + 1 < last_step)
    def _():
        idx_copy(s + 1, lax.rem(s + 1, 2)).start()


def _peer(scores_t, hn, hres, tab, n_first):
    n_tok, d = hn.shape
    assert n_first % EXPERT_TOKENS == 0
    first_blocks = n_first // EXPERT_TOKENS
    nsk = scores_t.shape[0]
    assert tab.shape[1:] == (ROW_WORDS, LANES) and d == ROW_WORDS * LANES
    assert n_tok % TOPK_TILE == 0 and STEPS_PER_TILE in (PEER_HEADS, 2 * PEER_HEADS)
    nstep = n_tok // EXPERT_TOKENS
    ntile = n_tok // TOPK_TILE
    cpos, cmask = _cand_constants(TOPK_TILE)
    prev = lambda s: (jnp.maximum(s - 1, 0), 0)
    tokp = pl.BlockSpec((EXPERT_TOKENS, d), prev)
    const = pl.BlockSpec((CAND_ROWS, TOPK_TILE), lambda s: (0, 0))
    tile_f32 = pltpu.VMEM((PEER_SLOTS, TOPK_TILE), F32)
    buf = pltpu.VMEM((EXPERT_TOKENS * PEER_SLOTS * ROW_WORDS, LANES), I32)
    return pl.pallas_call(
        functools.partial(_peer_kernel, last_step=nstep, ntile=ntile, first_blocks=first_blocks), grid=(nstep + 1,),
        out_shape=(jax.ShapeDtypeStruct((n_first, d), F32), jax.ShapeDtypeStruct((n_tok - n_first, d), F32)),
        in_specs=[pl.BlockSpec((nsk, N_SUBKEYS, TOPK_TILE), lambda s: (0, 0, 0)),
                  pl.BlockSpec((nsk, N_SUBKEYS, TOPK_TILE),
                               lambda s: (0, 0, jnp.minimum(s // STEPS_PER_TILE + 1, ntile - 1))),
                  const, const, tokp, tokp, pl.BlockSpec(memory_space=pl.ANY)],
        out_specs=_pair_specs(EXPERT_TOKENS, d, first_blocks, shift=1),
        scratch_shapes=[buf, buf, pltpu.SemaphoreType.DMA((2,)),
                        tile_f32, tile_f32, tile_f32, tile_f32, tile_f32, tile_f32,
                        pltpu.VMEM((2, TOPK_TILE, PEER_SLOTS), I32), pltpu.VMEM((2, TOPK_TILE, PEER_SLOTS), F32),
                        pltpu.SMEM((2, EXPERT_TOKENS, PEER_SLOTS), I32), pltpu.SemaphoreType.DMA((2,))],
        name="peer",
        compiler_params=pltpu.CompilerParams(dimension_semantics=("arbitrary",), vmem_limit_bytes=VMEM_LIMIT),
    )(scores_t, scores_t, cpos, cmask, hn, hres, tab)


def _layer(xa, xb, seqs, norm1_g, w_in, b_igate, b_fgate, mlstm_norm_g, qk_norm_q, qk_norm_k, rpb,
           w_branch_a, w_branch_b, w_out, norm2_g, peer_wq, peer_subkeys, peer_u, peer_v):
    tm = 256
    assert xa.shape[0] % tm == 0 and xb.shape[0] % tm == 0
    row = lambda a: a.astype(F32).reshape(1, -1)

    c0, c1, c2, c3 = 3 * M_WIDTH, 4 * M_WIDTH, 4 * M_WIDTH + 4 * M_HEADS, 4 * M_WIDTH + 4 * M_HEADS + 3 * N_WIDTH
    wm = w_in[:, 0:c1].astype(BF16)
    wg = jnp.pad(w_in[:, c1:c2], ((0, 0), (0, LANES - 4 * M_HEADS))).astype(BF16)
    wn = w_in[:, c2:c3].astype(BF16)
    wmg = w_in[:, c3:].astype(BF16)
    head_of = np.arange(N_WIDTH) // N_HEAD_DIM
    bd = jnp.asarray((head_of[:, None] == head_of[None, :]).astype(np.float32) / N_HEAD_DIM, BF16)
    gq = jnp.tile(qk_norm_q.astype(F32), N_HEADS).reshape(1, N_WIDTH)
    gk = jnp.tile(qk_norm_k.astype(F32), N_HEADS).reshape(1, N_WIDTH)
    gate_bias = jnp.pad(jnp.concatenate([b_igate.reshape(-1), b_fgate.reshape(-1)]).astype(F32),
                        (0, LANES - 4 * M_HEADS)).reshape(1, LANES)
    bias_tab = _natten_bias_table(rpb)
    half = PEER_KEY_DIM // 2
    sk = peer_subkeys.astype(F32).reshape(2 * PEER_HEADS, N_SUBKEYS, half)
    z = jnp.zeros_like(sk)
    odd = (np.arange(2 * PEER_HEADS) % 2 == 1)[:, None, None]
    skp = jnp.where(odd, jnp.concatenate([z, sk], axis=-1), jnp.concatenate([sk, z], axis=-1)).astype(BF16)
    uv = _pack_table(peer_u, peer_v)

    seq_chunks = tuple((s // M_CHUNK, (s + t) // M_CHUNK) for s, t in seqs)
    seq_blocks = tuple((s // NA_BLOCK, (s + t) // NA_BLOCK) for s, t in seqs)

    mq, mk, mv, mo, gates, nq, nk, nv, merge = _proj(xa, xb, row(norm1_g), wm, wg, wn, wmg, bd, gq, gk, tm)
    hf, hb = _mlstm(mq, mk, mv, gates, gate_bias, seq_chunks)
    na = _natten(nq, nk, nv, bias_tab, seq_blocks)
    hres, hn, scores_t = _mix(xa, xb, hf, hb, mo, na, merge, row(mlstm_norm_g), w_branch_a.astype(BF16),
                              w_branch_b.astype(BF16), w_out.astype(BF16), row(norm2_g), peer_wq.astype(BF16), skp, tm)
    return _peer(scores_t, hn, hres, uv, xa.shape[0])


def kernel(x_prompt, x_sample, norm1_g, w_in, b_igate, b_fgate, mlstm_norm_g, qk_norm_q, qk_norm_k, rpb,
           w_branch_a, w_branch_b, w_out, norm2_g, peer_wq, peer_subkeys, peer_u, peer_v):
    depth = w_in.shape[0]
    d = x_prompt.shape[-1]
    groups = (x_prompt, x_sample)
    seqs, start = [], 0
    for gx in groups:
        for _ in range(gx.shape[0]):
            seqs.append((start, gx.shape[1]))
            start += gx.shape[1]
    for s, t in seqs:
        assert t % NA_BLOCK == 0 and t % M_CHUNK == 0 and t // GRID_W >= WIN_H
    xa, xb = (gx.reshape(-1, d) for gx in groups)
    for l in range(depth):
        xa, xb = _layer(xa, xb, tuple(seqs), norm1_g[l], w_in[l], b_igate[l], b_fgate[l], mlstm_norm_g[l],
                        qk_norm_q[l], qk_norm_k[l], rpb[l], w_branch_a[l], w_branch_b[l], w_out[l], norm2_g[l],
                        peer_wq[l], peer_subkeys[l], peer_u[l], peer_v[l])
    return (xa.reshape(x_prompt.shape), xb.reshape(x_sample.shape))
```
